```python
import jax, jax.numpy as jnp
from jax import lax
import numpy as np

D_MODEL = 1024
BATCH = 2
SEQ = 8192
DEPTH = 2

CHUNK = 64
N_MEM = 256
CONV_WIDTH = 4
D_LRU = D_MODEL
LRU_BLOCKS = 8
LRU_BLOCK = D_LRU // LRU_BLOCKS
LRU_C = 8.0
D_SSD = 2 * D_MODEL
SSD_HEAD_DIM = 64
SSD_HEADS = D_SSD // SSD_HEAD_DIM
SSD_GROUPS = 4
SSD_HEADS_PER_GROUP = SSD_HEADS // SSD_GROUPS
SSD_STATE = 128
D_BC = SSD_GROUPS * SSD_STATE
D_XBC = D_SSD + 2 * D_BC
XA_HEADS = 4
XA_HEAD_DIM = 256
D_XA = XA_HEADS * XA_HEAD_DIM
N_BRANCH = 3
D_FF = ((8 * D_MODEL // 3 + 255) // 256) * 256
ALPHA = (2 * DEPTH) ** 0.25
BETA = (8 * DEPTH) ** -0.25
EPS = 1e-5

_SPLITS = (D_LRU, D_LRU, D_SSD, D_XBC, SSD_HEADS, D_XA, N_BRANCH * D_MODEL)
N_IN = sum(_SPLITS)
_OFFSETS = tuple(sum(_SPLITS[:i + 1]) for i in range(len(_SPLITS) - 1))

kernel_name = 'hybrid_rglru_ssd_memxattn_deepnorm'


def layer_norm(x, g, b):
    xf = x.astype(jnp.float32)
    mu = jnp.mean(xf, axis=-1, keepdims=True)
    var = jnp.mean(jnp.square(xf - mu), axis=-1, keepdims=True)
    return ((xf - mu) * lax.rsqrt(var + EPS) * g + b).astype(x.dtype)


def causal_depthwise_conv(x, w, b):
    c = x.shape[-1]
    y = lax.conv_general_dilated(
        x, w[:, None, :].astype(x.dtype), window_strides=(1,),
        padding=[(CONV_WIDTH - 1, 0)], dimension_numbers=('NWC', 'WIO', 'NWC'),
        feature_group_count=c)
    return y + b


def rg_lru(x, w_a, b_a, w_i, b_i, lam):
    xf = x.astype(jnp.float32)
    xb = xf.reshape(*xf.shape[:-1], LRU_BLOCKS, LRU_BLOCK)
    r = jax.nn.sigmoid(jnp.einsum('bsnk,nkj->bsnj', xb, w_a.astype(jnp.float32)).reshape(xf.shape) + b_a)
    i = jax.nn.sigmoid(jnp.einsum('bsnk,nkj->bsnj', xb, w_i.astype(jnp.float32)).reshape(xf.shape) + b_i)
    log_a = -LRU_C * r * jax.nn.softplus(-lam.astype(jnp.float32))
    a = jnp.exp(log_a)
    u = jnp.sqrt(-jnp.expm1(2.0 * log_a)) * (i * xf)

    def combine(lhs, rhs):
        a1, b1 = lhs
        a2, b2 = rhs
        return a1 * a2, a2 * b1 + b2

    _, h = lax.associative_scan(combine, (a, u), axis=1)
    return h


def ssd_chunked(xs, dt, a, bm, cm):
    bsz, s = xs.shape[:2]
    nc = s // CHUNK
    g, k, p, n = SSD_GROUPS, SSD_HEADS_PER_GROUP, SSD_HEAD_DIM, SSD_STATE
    x_c = (xs * dt[..., None]).reshape(bsz, nc, CHUNK, g, k, p)
    da = (dt * a).reshape(bsz, nc, CHUNK, g, k)
    b_c = bm.reshape(bsz, nc, CHUNK, g, n)
    c_c = cm.reshape(bsz, nc, CHUNK, g, n)
    cs = jnp.cumsum(da, axis=2)
    idx = jnp.arange(CHUNK)
    causal = (idx[:, None] >= idx[None, :])[:, :, None, None]
    seg = cs[:, :, :, None] - cs[:, :, None, :]
    decay = jnp.exp(jnp.where(causal, seg, -jnp.inf))
    cb = jnp.einsum('bclgn,bcsgn->bclsg', c_c, b_c)
    y_diag = jnp.einsum('bclsgk,bcsgkp->bclgkp', cb[..., None] * decay, x_c)
    decay_end = jnp.exp(cs[:, :, -1:] - cs)
    states = jnp.einsum('bclgn,bclgkp->bcgkpn', b_c, x_c * decay_end[..., None])
    chunk_decay = jnp.exp(cs[:, :, -1])

    def step(h, inp):
        st, dec = inp
        return h * dec[..., None, None] + st, h

    h0 = jnp.zeros((bsz, g, k, p, n), jnp.float32)
    _, prev = lax.scan(step, h0, (jnp.moveaxis(states, 1, 0), jnp.moveaxis(chunk_decay, 1, 0)))
    prev = jnp.moveaxis(prev, 0, 1)
    y_off = jnp.einsum('bclgn,bcgkpn->bclgkp', c_c, prev) * jnp.exp(cs)[..., None]
    return (y_diag + y_off).reshape(bsz, s, SSD_HEADS, p)


def ssd_branch(z, xbc, dt_raw, conv_w, conv_b, dt_bias, a_log, d_skip, norm_w):
    bsz, s = z.shape[:2]
    xbc = jax.nn.silu(causal_depthwise_conv(xbc, conv_w, conv_b)).astype(jnp.float32)
    xs, bm, cm = jnp.split(xbc, [D_SSD, D_SSD + D_BC], axis=-1)
    xs = xs.reshape(bsz, s, SSD_HEADS, SSD_HEAD_DIM)
    bm = bm.reshape(bsz, s, SSD_GROUPS, SSD_STATE)
    cm = cm.reshape(bsz, s, SSD_GROUPS, SSD_STATE)
    dt = jax.nn.softplus(dt_raw.astype(jnp.float32) + dt_bias)
    a = -jnp.exp(a_log.astype(jnp.float32))
    y = ssd_chunked(xs, dt, a, bm, cm) + d_skip[:, None] * xs
    y = y.reshape(bsz, s, D_SSD) * jax.nn.silu(z.astype(jnp.float32))
    yg = y.reshape(bsz, s, SSD_GROUPS, D_SSD // SSD_GROUPS)
    yg = yg * lax.rsqrt(jnp.mean(jnp.square(yg), axis=-1, keepdims=True) + EPS)
    return (yg.reshape(bsz, s, D_SSD) * norm_w).astype(z.dtype)


def memory_cross_attention(q, mem, w_kv):
    bsz, s = q.shape[:2]
    m = mem.shape[1]
    k, v = jnp.split(mem @ w_kv, 2, axis=-1)
    q = q.reshape(bsz, s, XA_HEADS, XA_HEAD_DIM)
    k = k.reshape(bsz, m, XA_HEADS, XA_HEAD_DIM)
    v = v.reshape(bsz, m, XA_HEADS, XA_HEAD_DIM)
    scores = jnp.einsum('bshd,bmhd->bhsm', q, k).astype(jnp.float32) * (XA_HEAD_DIM ** -0.5)
    probs = jax.nn.softmax(scores, axis=-1).astype(v.dtype)
    return jnp.einsum('bhsm,bmhd->bshd', probs, v).reshape(bsz, s, D_XA)


def hybrid_mixer(x, mem, w_in, b_gate, lru_conv_w, lru_conv_b, lru_w_a, lru_b_a, lru_w_i, lru_b_i,
                 lru_lambda, ssd_conv_w, ssd_conv_b, ssd_dt_bias, ssd_a_log, ssd_d, ssd_norm_w,
                 mem_w_kv, w_br_lru, w_br_ssd, w_br_xa, w_out):
    bsz, s = x.shape[:2]
    proj = x @ w_in
    lru_x, lru_gate, ssd_z, ssd_xbc, ssd_dt, xa_q, gate_logits = jnp.split(proj, _OFFSETS, axis=-1)
    h = rg_lru(causal_depthwise_conv(lru_x, lru_conv_w, lru_conv_b), lru_w_a, lru_b_a, lru_w_i, lru_b_i, lru_lambda)
    y_lru = (jax.nn.gelu(lru_gate.astype(jnp.float32)) * h).astype(x.dtype)
    y_ssd = ssd_branch(ssd_z, ssd_xbc, ssd_dt, ssd_conv_w, ssd_conv_b, ssd_dt_bias, ssd_a_log, ssd_d, ssd_norm_w)
    y_xa = memory_cross_attention(xa_q, mem, mem_w_kv)
    gates = jax.nn.sigmoid(gate_logits.reshape(bsz, s, N_BRANCH, D_MODEL) + b_gate)
    merged = (gates[:, :, 0] * (y_lru @ w_br_lru)
              + gates[:, :, 1] * (y_ssd @ w_br_ssd)
              + gates[:, :, 2] * (y_xa @ w_br_xa))
    return merged @ w_out


def swiglu(x, w_in, w_down):
    gate, up = jnp.split(x @ w_in, 2, axis=-1)
    return (jax.nn.silu(gate) * up) @ w_down


def setup_inputs(seed: int = 0) -> dict:
    key = jax.random.key(seed)
    ks = jax.random.split(key, 32)

    def nrm(k, shape, scale):
        return jax.random.normal(k, shape, jnp.float32) * scale

    a0 = jax.random.uniform(ks[10], (DEPTH, D_LRU), jnp.float32, 0.9, 0.999)
    root = a0 ** (1.0 / LRU_C)
    lru_lambda = jnp.log(root) - jnp.log1p(-root)
    dt0 = jnp.exp(jax.random.uniform(ks[13], (DEPTH, SSD_HEADS), jnp.float32, np.log(0.001), np.log(0.1)))
    ssd_dt_bias = dt0 + jnp.log(-jnp.expm1(-dt0))
    ssd_a_log = jnp.log(jax.random.uniform(ks[14], (DEPTH, SSD_HEADS), jnp.float32, 1.0, 16.0))
    return {
        'x': nrm(ks[0], (BATCH, SEQ, D_MODEL), 1.0),
        'mem': nrm(ks[1], (BATCH, N_MEM, D_MODEL), 1.0),
        'w_in': nrm(ks[2], (DEPTH, D_MODEL, N_IN), D_MODEL ** -0.5),
        'b_gate': nrm(ks[3], (DEPTH, N_BRANCH, D_MODEL), 0.1),
        'lru_conv_w': nrm(ks[4], (DEPTH, CONV_WIDTH, D_LRU), CONV_WIDTH ** -0.5),
        'lru_conv_b': nrm(ks[5], (DEPTH, D_LRU), 0.02),
        'lru_w_a': nrm(ks[6], (DEPTH, LRU_BLOCKS, LRU_BLOCK, LRU_BLOCK), LRU_BLOCK ** -0.5),
        'lru_b_a': nrm(ks[7], (DEPTH, D_LRU), 0.02),
        'lru_w_i': nrm(ks[8], (DEPTH, LRU_BLOCKS, LRU_BLOCK, LRU_BLOCK), LRU_BLOCK ** -0.5),
        'lru_b_i': nrm(ks[9], (DEPTH, D_LRU), 0.02),
        'lru_lambda': lru_lambda,
        'ssd_conv_w': nrm(ks[11], (DEPTH, CONV_WIDTH, D_XBC), CONV_WIDTH ** -0.5),
        'ssd_conv_b': nrm(ks[12], (DEPTH, D_XBC), 0.02),
        'ssd_dt_bias': ssd_dt_bias,
        'ssd_a_log': ssd_a_log,
        'ssd_d': 1.0 + nrm(ks[15], (DEPTH, SSD_HEADS), 0.02),
        'ssd_norm_w': 1.0 + nrm(ks[16], (DEPTH, D_SSD), 0.02),
        'mem_w_kv': nrm(ks[17], (DEPTH, D_MODEL, 2 * D_XA), D_MODEL ** -0.5),
        'w_br_lru': nrm(ks[18], (DEPTH, D_LRU, D_MODEL), D_LRU ** -0.5),
        'w_br_ssd': nrm(ks[19], (DEPTH, D_SSD, D_MODEL), D_SSD ** -0.5),
        'w_br_xa': nrm(ks[20], (DEPTH, D_XA, D_MODEL), D_XA ** -0.5),
        'w_out': nrm(ks[21], (DEPTH, D_MODEL, D_MODEL), BETA * D_MODEL ** -0.5),
        'ln1_g': 1.0 + nrm(ks[22], (DEPTH, D_MODEL), 0.02),
        'ln1_b': nrm(ks[23], (DEPTH, D_MODEL), 0.02),
        'ffn_w_in': nrm(ks[24], (DEPTH, D_MODEL, 2 * D_FF), D_MODEL ** -0.5),
        'ffn_w_down': nrm(ks[25], (DEPTH, D_FF, D_MODEL), BETA * D_FF ** -0.5),
        'ln2_g': 1.0 + nrm(ks[26], (DEPTH, D_MODEL), 0.02),
        'ln2_b': nrm(ks[27], (DEPTH, D_MODEL), 0.02),
    }


def reference(x, mem, w_in, b_gate, lru_conv_w, lru_conv_b, lru_w_a, lru_b_a, lru_w_i, lru_b_i,
              lru_lambda, ssd_conv_w, ssd_conv_b, ssd_dt_bias, ssd_a_log, ssd_d, ssd_norm_w,
              mem_w_kv, w_br_lru, w_br_ssd, w_br_xa, w_out, ln1_g, ln1_b, ffn_w_in, ffn_w_down,
              ln2_g, ln2_b):
    for l in range(DEPTH):
        mix = hybrid_mixer(x, mem, w_in[l], b_gate[l], lru_conv_w[l], lru_conv_b[l], lru_w_a[l], lru_b_a[l],
                           lru_w_i[l], lru_b_i[l], lru_lambda[l], ssd_conv_w[l], ssd_conv_b[l], ssd_dt_bias[l],
                           ssd_a_log[l], ssd_d[l], ssd_norm_w[l], mem_w_kv[l], w_br_lru[l], w_br_ssd[l],
                           w_br_xa[l], w_out[l])
        x = layer_norm(ALPHA * x + mix, ln1_g[l], ln1_b[l])
        x = layer_norm(ALPHA * x + swiglu(x, ffn_w_in[l], ffn_w_down[l]), ln2_g[l], ln2_b[l])
    return x
```

```python
import functools

import numpy as np
import jax
import jax.numpy as jnp
from jax import lax
from jax.experimental import pallas as pl
from jax.experimental.pallas import tpu as pltpu

F32 = jnp.float32
BF16 = jnp.bfloat16

D_MODEL = 1024
CONV_WIDTH = 4
D_LRU = D_MODEL
LRU_BLOCKS = 8
LRU_BLOCK = D_LRU // LRU_BLOCKS
LRU_C = 8.0
D_SSD = 2 * D_MODEL
SSD_HEAD_DIM = 64
SSD_HEADS = D_SSD // SSD_HEAD_DIM
SSD_GROUPS = 4
SSD_STATE = 128
D_BC = SSD_GROUPS * SSD_STATE
D_XBC = D_SSD + 2 * D_BC
D_GROUP = D_SSD // SSD_GROUPS
XA_HEADS = 4
XA_HEAD_DIM = 256
D_XA = XA_HEADS * XA_HEAD_DIM
N_BRANCH = 3
D_FF = ((8 * D_MODEL // 3 + 255) // 256) * 256
EPS = 1e-5

SUBLANES = 8
LANES = 128
HALO = SUBLANES
SSD_L = 128
DT_PAD = LANES

LRU_TC = 512
SSD_TC = 256
MIX_TM = 256
FFN_TM = 256
VMEM_LIMIT = 56 * 1024 * 1024


def _resident(shape):
    nd = len(shape)
    return pl.BlockSpec(shape, lambda *_: (0,) * nd, pipeline_mode=pl.Buffered(1))


def _sigmoid(x):
    return 1.0 / (1.0 + jnp.exp(-x))


def _silu(x):
    return x * _sigmoid(x)


def _softplus(x):
    return jnp.maximum(x, 0.0) + jnp.log1p(jnp.exp(-jnp.abs(x)))


def _gelu_tanh(x):
    return 0.5 * x * (1.0 + jnp.tanh(0.7978845608028654 * (x + 0.044715 * (x * x * x))))


def _layer_norm(v, g, b):
    mu = jnp.mean(v, axis=-1, keepdims=True)
    c = v - mu
    var = jnp.mean(c * c, axis=-1, keepdims=True)
    return c * lax.rsqrt(var + EPS) * g + b


def _causal_conv(ext_ref, cur, cw, cb, rows):
    ext_ref[HALO:HALO + rows, :] = cur
    y = cb + cw[3:4, :] * cur
    for k in range(CONV_WIDTH - 1):
        y = y + cw[k:k + 1, :] * ext_ref[HALO - 3 + k:HALO - 3 + k + rows, :]
    ext_ref[0:HALO, :] = cur[rows - HALO:rows, :]
    return y


def _split_bf16(v, terms):
    parts = []
    rem = v
    for _ in range(terms):
        p = rem.astype(BF16)
        parts.append(p)
        rem = rem - p.astype(F32)
    return jnp.concatenate(parts, axis=1)


def _kv_kernel(mem_ref, w_ref, o_ref):
    o_ref[...] = jnp.dot(mem_ref[...].astype(BF16), w_ref[...], preferred_element_type=F32).astype(BF16)


def _kv_proj(mem, w_kv):
    b, m, d = mem.shape
    n = w_kv.shape[1]
    return pl.pallas_call(
        _kv_kernel,
        grid=(b,),
        in_specs=[pl.BlockSpec((None, m, d), lambda i: (i, 0, 0)), _resident((d, n))],
        out_specs=pl.BlockSpec((None, m, n), lambda i: (i, 0, 0)),
        out_shape=jax.ShapeDtypeStruct((b, m, n), BF16),
        compiler_params=pltpu.CompilerParams(dimension_semantics=("arbitrary",), vmem_limit_bytes=VMEM_LIMIT),
        name="kv_proj",
    )(mem, w_kv)


def _lru_kernel(x_ref, w_ref, cw_ref, cb_ref, wai_ref, ba_ref, bi_ref, lam_ref, o_ref, ext_ref, h_ref, *, rows):
    @pl.when(pl.program_id(1) == 0)
    def _():
        ext_ref[0:HALO, :] = jnp.zeros((HALO, D_LRU), F32)
        h_ref[...] = jnp.zeros_like(h_ref)

    proj = jnp.dot(x_ref[...].astype(BF16), w_ref[...], preferred_element_type=F32)
    xc = _causal_conv(ext_ref, proj[:, :D_LRU], cw_ref[...], cb_ref[...], rows)
    xcb = xc.astype(BF16)
    neg_c_sp = -LRU_C * _softplus(-lam_ref[...])
    groups = rows // SUBLANES
    sub = lax.broadcasted_iota(jnp.int32, (groups, SUBLANES, LRU_BLOCK), 1)

    for n in range(LRU_BLOCKS):
        lanes = slice(n * LRU_BLOCK, (n + 1) * LRU_BLOCK)
        ai = jnp.dot(xcb[:, lanes], wai_ref[n], preferred_element_type=F32)
        r = _sigmoid(ai[:, :LRU_BLOCK] + ba_ref[:, lanes])
        i = _sigmoid(ai[:, LRU_BLOCK:] + bi_ref[:, lanes])
        log_a = neg_c_sp[:, lanes] * r
        a = jnp.exp(log_a)
        u = jnp.sqrt(1.0 - a * a) * (i * xc[:, lanes])
        a3 = a.reshape(groups, SUBLANES, LRU_BLOCK)
        b3 = u.reshape(groups, SUBLANES, LRU_BLOCK)
        d = 1
        while d < SUBLANES:
            a_prev = pltpu.roll(a3, d, 1)
            b_prev = pltpu.roll(b3, d, 1)
            keep = sub >= d
            b3 = jnp.where(keep, a3 * b_prev + b3, b3)
            a3 = jnp.where(keep, a3 * a_prev, a3)
            d *= 2
        hp = h_ref[:, lanes]
        hs = []
        for g in range(groups):
            hg = a3[g] * hp + b3[g]
            hp = hg[SUBLANES - 1:SUBLANES, :]
            hs.append(hg)
        h_ref[:, lanes] = hp
        h = jnp.stack(hs, axis=0).reshape(rows, LRU_BLOCK)
        gate = proj[:, D_LRU + n * LRU_BLOCK:D_LRU + (n + 1) * LRU_BLOCK]
        o_ref[:, lanes] = (_gelu_tanh(gate) * h).astype(BF16)


def _lru_branch(x, w, cw, cb, wai, ba, bi, lam):
    b, s, d = x.shape
    rows = min(LRU_TC, s)
    return pl.pallas_call(
        functools.partial(_lru_kernel, rows=rows),
        grid=(b, s // rows),
        in_specs=[
            pl.BlockSpec((None, rows, d), lambda i, j: (i, j, 0)),
            _resident(w.shape), _resident(cw.shape), _resident(cb.shape), _resident(wai.shape),
            _resident(ba.shape), _resident(bi.shape), _resident(lam.shape),
        ],
        out_specs=pl.BlockSpec((None, rows, D_LRU), lambda i, j: (i, j, 0)),
        out_shape=jax.ShapeDtypeStruct((b, s, D_LRU), BF16),
        scratch_shapes=[pltpu.VMEM((HALO + rows, D_LRU), F32), pltpu.VMEM((1, D_LRU), F32)],
        compiler_params=pltpu.CompilerParams(dimension_semantics=("arbitrary", "arbitrary"),
                                             vmem_limit_bytes=VMEM_LIMIT),
        name="lru_branch",
    )(x, w, cw, cb, wai, ba, bi, lam)


def _ssd_constants():
    length = SSD_L
    tri = np.tril(np.ones((length, length), np.float32))
    ecol = np.zeros((3 * LANES, SSD_HEADS * length), np.float32)
    ehead = np.zeros((2 * LANES, D_SSD), np.float32)
    for h in range(SSD_HEADS):
        for t in range(3):
            ecol[t * LANES + h, h * length:(h + 1) * length] = 1.0
        for t in range(2):
            ehead[t * LANES + h, h * SSD_HEAD_DIM:(h + 1) * SSD_HEAD_DIM] = 1.0
    return jnp.asarray(tri, BF16), jnp.asarray(ecol, BF16), jnp.asarray(ehead, BF16)


def _ssd_kernel(x_ref, w_ref, cw_ref, cb_ref, dtb_ref, alog_ref, dskip_ref, nw_ref, tri_ref, ecol_ref, ehead_ref,
                o_ref, ext_ref, state_ref, *, rows):
    @pl.when(pl.program_id(1) == 0)
    def _():
        ext_ref[0:HALO, :] = jnp.zeros((HALO, D_XBC), F32)
        state_ref[...] = jnp.zeros_like(state_ref)

    length = SSD_L
    proj = jnp.dot(x_ref[...].astype(BF16), w_ref[...], preferred_element_type=F32)
    xbc = _silu(_causal_conv(ext_ref, proj[:, D_SSD:D_SSD + D_XBC], cw_ref[...], cb_ref[...], rows))
    dt = _softplus(proj[:, D_SSD + D_XBC:] + dtb_ref[...])
    da = dt * (-jnp.exp(alog_ref[...]))
    r_i = lax.broadcasted_iota(jnp.int32, (length, length), 0)
    c_i = lax.broadcasted_iota(jnp.int32, (length, length), 1)
    causal = r_i >= c_i
    lane = lax.broadcasted_iota(jnp.int32, (length, LANES), 1)
    first_head = lane < SSD_HEAD_DIM
    tri = tri_ref[...]
    ecol = ecol_ref[...]
    ehead = ehead_ref[...]

    for c in range(rows // length):
        rs = slice(c * length, (c + 1) * length)
        xs = xbc[rs, :D_SSD]
        dt_c = dt[rs, :]
        cs3 = jnp.dot(tri, _split_bf16(da[rs, :], 3), preferred_element_type=F32)
        cs = cs3[:, :LANES] + cs3[:, LANES:2 * LANES] + cs3[:, 2 * LANES:]
        cs_last = cs[length - 1:length, :]
        w_state = dt_c * jnp.exp(cs_last - cs)
        w_out = jnp.exp(cs)
        cs_t = cs.T
        dt_t = dt_c.T
        col = jnp.dot(_split_bf16(cs, 3), ecol, preferred_element_type=F32)
        w_state_e = jnp.dot(_split_bf16(w_state, 2), ehead, preferred_element_type=F32)
        w_out_e = jnp.dot(_split_bf16(w_out, 2), ehead, preferred_element_type=F32)

        y_cols = []
        for g in range(SSD_GROUPS):
            tiles = []
            b_g = xbc[rs, D_SSD + g * SSD_STATE:D_SSD + (g + 1) * SSD_STATE]
            c_g = xbc[rs, D_SSD + D_BC + g * SSD_STATE:D_SSD + D_BC + (g + 1) * SSD_STATE].astype(BF16)
            cb = lax.dot_general(c_g, b_g.astype(BF16), (((1,), (1,)), ((), ())), preferred_element_type=F32)
            gl = slice(g * D_GROUP, (g + 1) * D_GROUP)
            heads_per_group = SSD_HEADS // SSD_GROUPS
            for p in range(heads_per_group // 2):
                h0 = g * heads_per_group + 2 * p
                ms = []
                for h in (h0, h0 + 1):
                    seg = col[:, h * length:(h + 1) * length] - cs_t[h:h + 1, :]
                    dec = jnp.where(causal, jnp.exp(jnp.minimum(seg, 0.0)), 0.0)
                    ms.append((cb * dec * dt_t[h:h + 1, :]).astype(BF16))
                x_t = xs[:, h0 * SSD_HEAD_DIM:(h0 + 2) * SSD_HEAD_DIM]
                rhs = jnp.concatenate([jnp.where(first_head, x_t, 0.0), jnp.where(first_head, 0.0, x_t)],
                                      axis=0).astype(BF16)
                tiles.append(jnp.dot(jnp.concatenate(ms, axis=1), rhs, preferred_element_type=F32))
            st = state_ref[g]
            y_off = jnp.dot(c_g, st.astype(BF16), preferred_element_type=F32) * w_out_e[:, gl]
            xd = (xs[:, gl] * w_state_e[:, gl]).astype(BF16)
            s_new = jnp.dot(b_g.T.astype(BF16), xd, preferred_element_type=F32)
            state_ref[g] = st * w_out_e[length - 1:length, gl] + s_new
            y_cols.append(jnp.concatenate(tiles, axis=1) + y_off)

        y = jnp.concatenate(y_cols, axis=1) + dskip_ref[...] * xs
        y = y * _silu(proj[rs, :D_SSD])
        outs = []
        for g in range(SSD_GROUPS):
            yg = y[:, g * D_GROUP:(g + 1) * D_GROUP]
            outs.append(yg * lax.rsqrt(jnp.mean(yg * yg, axis=-1, keepdims=True) + EPS))
        o_ref[rs, :] = (jnp.concatenate(outs, axis=1) * nw_ref[...]).astype(BF16)


def _ssd_branch(x, w, cw, cb, dtb, alog, dskip, nw):
    b, s, d = x.shape
    rows = min(SSD_TC, s)
    consts = _ssd_constants()
    args = (x, w, cw, cb, dtb, alog, dskip, nw) + consts
    return pl.pallas_call(
        functools.partial(_ssd_kernel, rows=rows),
        grid=(b, s // rows),
        in_specs=[pl.BlockSpec((None, rows, d), lambda i, j: (i, j, 0))] + [_resident(a.shape) for a in args[1:]],
        out_specs=pl.BlockSpec((None, rows, D_SSD), lambda i, j: (i, j, 0)),
        out_shape=jax.ShapeDtypeStruct((b, s, D_SSD), BF16),
        scratch_shapes=[pltpu.VMEM((HALO + rows, D_XBC), F32),
                        pltpu.VMEM((SSD_GROUPS, SSD_STATE, D_GROUP), F32)],
        compiler_params=pltpu.CompilerParams(dimension_semantics=("arbitrary", "arbitrary"),
                                             vmem_limit_bytes=VMEM_LIMIT),
        name="ssd_branch",
    )(*args)


def _mix_kernel(x_ref, wqg_ref, kv_ref, ylru_ref, yssd_ref, bg_ref, wl_ref, ws_ref, wx_ref, wo_ref, g_ref, b_ref,
                o_ref, *, alpha):
    xf = x_ref[...]
    pq = jnp.dot(xf.astype(BF16), wqg_ref[...], preferred_element_type=F32)
    heads = []
    for h in range(XA_HEADS):
        q_h = pq[:, h * XA_HEAD_DIM:(h + 1) * XA_HEAD_DIM].astype(BF16)
        k_h = kv_ref[:, h * XA_HEAD_DIM:(h + 1) * XA_HEAD_DIM]
        v_h = kv_ref[:, D_XA + h * XA_HEAD_DIM:D_XA + (h + 1) * XA_HEAD_DIM]
        sc = lax.dot_general(q_h, k_h, (((1,), (1,)), ((), ())), preferred_element_type=F32) * (XA_HEAD_DIM ** -0.5)
        e = jnp.exp(sc - jnp.max(sc, axis=-1, keepdims=True))
        probs = e / jnp.sum(e, axis=-1, keepdims=True)
        heads.append(jnp.dot(probs.astype(BF16), v_h, preferred_element_type=F32))
    y_xa = jnp.concatenate(heads, axis=1).astype(BF16)
    gates = _sigmoid(pq[:, D_XA:] + bg_ref[...])
    merged = (gates[:, :D_MODEL] * jnp.dot(ylru_ref[...], wl_ref[...], preferred_element_type=F32)
              + gates[:, D_MODEL:2 * D_MODEL] * jnp.dot(yssd_ref[...], ws_ref[...], preferred_element_type=F32)
              + gates[:, 2 * D_MODEL:] * jnp.dot(y_xa, wx_ref[...], preferred_element_type=F32))
    mix = jnp.dot(merged.astype(BF16), wo_ref[...], preferred_element_type=F32)
    o_ref[...] = _layer_norm(alpha * xf + mix, g_ref[...], b_ref[...])


def _mix_layer(x, wqg, kv, ylru, yssd, bg, wl, ws, wx, wo, g, bb, alpha):
    b, s, d = x.shape
    rows = min(MIX_TM, s)
    m = kv.shape[1]
    tok = lambda width: pl.BlockSpec((None, rows, width), lambda i, j: (i, j, 0))
    return pl.pallas_call(
        functools.partial(_mix_kernel, alpha=alpha),
        grid=(b, s // rows),
        in_specs=[tok(d), _resident(wqg.shape), pl.BlockSpec((None, m, kv.shape[2]), lambda i, j: (i, 0, 0)),
                  tok(D_LRU), tok(D_SSD), _resident(bg.shape), _resident(wl.shape), _resident(ws.shape),
                  _resident(wx.shape), _resident(wo.shape), _resident(g.shape), _resident(bb.shape)],
        out_specs=tok(d),
        out_shape=jax.ShapeDtypeStruct((b, s, d), F32),
        compiler_params=pltpu.CompilerParams(dimension_semantics=("arbitrary", "arbitrary"),
                                             vmem_limit_bytes=VMEM_LIMIT),
        name="mix_layer",
    )(x, wqg, kv, ylru, yssd, bg, wl, ws, wx, wo, g, bb)


def _ffn_kernel(x_ref, wi_ref, wd_ref, g_ref, b_ref, o_ref, *, alpha):
    xf = x_ref[...]
    hid = jnp.dot(xf.astype(BF16), wi_ref[...], preferred_element_type=F32)
    act = (_silu(hid[:, :D_FF]) * hid[:, D_FF:]).astype(BF16)
    down = jnp.dot(act, wd_ref[...], preferred_element_type=F32)
    o_ref[...] = _layer_norm(alpha * xf + down, g_ref[...], b_ref[...])


def _ffn_layer(x, wi, wd, g, bb, alpha):
    b, s, d = x.shape
    rows = min(FFN_TM, s)
    tok = pl.BlockSpec((None, rows, d), lambda i, j: (i, j, 0))
    return pl.pallas_call(
        functools.partial(_ffn_kernel, alpha=alpha),
        grid=(b, s // rows),
        in_specs=[tok, _resident(wi.shape), _resident(wd.shape), _resident(g.shape), _resident(bb.shape)],
        out_specs=tok,
        out_shape=jax.ShapeDtypeStruct((b, s, d), F32),
        compiler_params=pltpu.CompilerParams(dimension_semantics=("arbitrary", "arbitrary"),
                                             vmem_limit_bytes=VMEM_LIMIT),
        name="ffn_layer",
    )(x, wi, wd, g, bb)


def kernel(x, mem, w_in, b_gate, lru_conv_w, lru_conv_b, lru_w_a, lru_b_a, lru_w_i, lru_b_i, lru_lambda, ssd_conv_w,
           ssd_conv_b, ssd_dt_bias, ssd_a_log, ssd_d, ssd_norm_w, mem_w_kv, w_br_lru, w_br_ssd, w_br_xa, w_out, ln1_g,
           ln1_b, ffn_w_in, ffn_w_down, ln2_g, ln2_b):
    depth = w_in.shape[0]
    alpha = (2 * depth) ** 0.25
    o_z = 2 * D_LRU
    o_xbc = o_z + D_SSD
    o_dt = o_xbc + D_XBC
    o_q = o_dt + SSD_HEADS

    w_lru = w_in[:, :, :o_z].astype(BF16)
    w_ssd = jnp.concatenate(
        [w_in[:, :, o_z:o_dt], jnp.pad(w_in[:, :, o_dt:o_q], ((0, 0), (0, 0), (0, DT_PAD - SSD_HEADS)))],
        axis=-1).astype(BF16)
    w_qg = w_in[:, :, o_q:].astype(BF16)
    w_ai = jnp.concatenate([lru_w_a, lru_w_i], axis=-1).astype(BF16)
    pad_heads = lambda a: jnp.pad(a, ((0, 0), (0, DT_PAD - SSD_HEADS)))[:, None, :]
    dtb = pad_heads(ssd_dt_bias)
    alog = pad_heads(ssd_a_log)
    dskip = jnp.repeat(ssd_d, SSD_HEAD_DIM, axis=-1)[:, None, :]
    row = lambda a: a[:, None, :]
    w_kv = mem_w_kv.astype(BF16)
    wl, ws, wx, wo = (w.astype(BF16) for w in (w_br_lru, w_br_ssd, w_br_xa, w_out))
    wi, wd = ffn_w_in.astype(BF16), ffn_w_down.astype(BF16)
    bg = b_gate.reshape(depth, 1, N_BRANCH * D_MODEL)

    for l in range(depth):
        kv = _kv_proj(mem, w_kv[l])
        y_lru = _lru_branch(x, w_lru[l], lru_conv_w[l], row(lru_conv_b)[l], w_ai[l], row(lru_b_a)[l],
                            row(lru_b_i)[l], row(lru_lambda)[l])
        y_ssd = _ssd_branch(x, w_ssd[l], ssd_conv_w[l], row(ssd_conv_b)[l], dtb[l], alog[l], dskip[l],
                            row(ssd_norm_w)[l])
        x = _mix_layer(x, w_qg[l], kv, y_lru, y_ssd, bg[l], wl[l], ws[l], wx[l], wo[l], row(ln1_g)[l],
                       row(ln1_b)[l], alpha)
        x = _ffn_layer(x, wi[l], wd[l], row(ln2_g)[l], row(ln2_b)[l], alpha)
    return x
```

```python
import functools

import numpy as np
import jax
import jax.numpy as jnp
from jax import lax
from jax.experimental import pallas as pl
from jax.experimental.pallas import tpu as pltpu

F32 = jnp.float32
BF16 = jnp.bfloat16

D_MODEL = 1024
CONV_WIDTH = 4
D_LRU = D_MODEL
LRU_BLOCKS = 8
LRU_BLOCK = D_LRU // LRU_BLOCKS
LRU_C = 8.0
D_SSD = 2 * D_MODEL
SSD_HEAD_DIM = 64
SSD_HEADS = D_SSD // SSD_HEAD_DIM
SSD_GROUPS = 4
SSD_STATE = 128
D_BC = SSD_GROUPS * SSD_STATE
D_XBC = D_SSD + 2 * D_BC
D_GROUP = D_SSD // SSD_GROUPS
XA_HEADS = 4
XA_HEAD_DIM = 256
D_XA = XA_HEADS * XA_HEAD_DIM
N_BRANCH = 3
D_FF = ((8 * D_MODEL // 3 + 255) // 256) * 256
EPS = 1e-5

SUBLANES = 8
LANES = 128
PERM = 128
SEG = PERM // SUBLANES
HALO = (CONV_WIDTH - 1) * SUBLANES
DT_PAD = LANES

LRU_TC = 512
SSD_TC = 256
MIX_TM = 512
MIX_SUB = 256
FFN_TM = 512
FFN_SUB = 256
VMEM_LIMIT = 56 * 1024 * 1024


def _resident(shape):
    nd = len(shape)
    return pl.BlockSpec(shape, lambda *_: (0,) * nd, pipeline_mode=pl.Buffered(1))


def _layer(arr, layer, block=None):
    shape = tuple(arr.shape[1:] if block is None else block)
    return pl.BlockSpec((None,) + shape, lambda *_: (layer,) + (0,) * len(shape), pipeline_mode=pl.Buffered(1))


def _sigmoid(x):
    return 1.0 / (1.0 + jnp.exp(-x))


def _silu(x):
    return x * _sigmoid(x)


def _softplus(x):
    return jnp.maximum(x, 0.0) + jnp.log1p(jnp.exp(-jnp.abs(x)))


def _gelu_tanh(x):
    return 0.5 * x * (1.0 + jnp.tanh(0.7978845608028654 * (x + 0.044715 * (x * x * x))))


def _layer_norm(v, g, b):
    mu = jnp.mean(v, axis=-1, keepdims=True)
    c = v - mu
    var = jnp.mean(c * c, axis=-1, keepdims=True)
    return c * lax.rsqrt(var + EPS) * g + b


def _perm_matrices():
    p = np.zeros((PERM, PERM), np.float32)
    for pos in range(PERM):
        j, m = divmod(pos, SUBLANES)
        p[pos, SEG * m + j] = 1.0
    return jnp.asarray(p, BF16), jnp.asarray(p.T, BF16)


def _permute_rows(perm, v):
    blocks = [jnp.dot(perm, v[b * PERM:(b + 1) * PERM, :], preferred_element_type=F32).astype(BF16)
              for b in range(v.shape[0] // PERM)]
    return blocks[0] if len(blocks) == 1 else jnp.concatenate(blocks, axis=0)


def _perm_conv(cur, prev_rolled, cw, cb):
    c = cur.shape[1]
    taps = CONV_WIDTH - 1
    rolled = pltpu.roll(cur[PERM - HALO:, :].reshape(taps, SUBLANES, c), 1, 1)
    sub = lax.broadcasted_iota(jnp.int32, (taps, SUBLANES, c), 1)
    halo = jnp.where(sub == 0, prev_rolled.reshape(taps, SUBLANES, c), rolled).reshape(HALO, c)
    ext = jnp.concatenate([halo, cur], axis=0)
    y = cb + cw[taps:taps + 1, :] * cur
    for k in range(1, CONV_WIDTH):
        y = y + cw[taps - k:taps - k + 1, :] * ext[HALO - SUBLANES * k:HALO - SUBLANES * k + PERM, :]
    return y, rolled.reshape(HALO, c)


def _split_bf16(v, terms):
    parts = []
    rem = v
    for _ in range(terms):
        p = rem.astype(BF16)
        parts.append(p)
        rem = rem - p.astype(F32)
    return jnp.concatenate(parts, axis=1)


def _kv_kernel(mem_ref, w_ref, o_ref):
    o_ref[...] = jnp.dot(mem_ref[...].astype(BF16), w_ref[...], preferred_element_type=F32).astype(BF16)


def _kv_proj(mem, w_kv, layer):
    b, m, d = mem.shape
    n = w_kv.shape[2]
    return pl.pallas_call(
        _kv_kernel,
        grid=(b,),
        in_specs=[pl.BlockSpec((None, m, d), lambda i: (i, 0, 0)), _layer(w_kv, layer)],
        out_specs=pl.BlockSpec((None, m, n), lambda i: (i, 0, 0)),
        out_shape=jax.ShapeDtypeStruct((b, m, n), BF16),
        compiler_params=pltpu.CompilerParams(dimension_semantics=("arbitrary",), vmem_limit_bytes=VMEM_LIMIT),
        name="kv_proj",
    )(mem, w_kv)


def _perm_scan(a, u, h_in):
    width = a.shape[1]
    a3 = a.reshape(SEG, SUBLANES, width)
    u3 = u.reshape(SEG, SUBLANES, width)
    h_loc = [u3[0]]
    p_loc = [a3[0]]
    for j in range(1, SEG):
        h_loc.append(a3[j] * h_loc[-1] + u3[j])
        p_loc.append(a3[j] * p_loc[-1])
    ends_h, ends_p = h_loc[-1], p_loc[-1]
    carry = h_in
    starts = []
    for m in range(SUBLANES):
        starts.append(carry)
        carry = ends_h[m:m + 1, :] + ends_p[m:m + 1, :] * carry
    start = jnp.concatenate(starts, axis=0)
    h = jnp.stack([h_loc[j] + p_loc[j] * start for j in range(SEG)], axis=0)
    return h.reshape(PERM, width), carry


def _pipelined_steps(step, proj_a, proj_b):
    s = pl.program_id(0)

    @pl.when(s == 0)
    def _():
        proj_b[...] = jnp.zeros_like(proj_b)

    @pl.when(s % 2 == 0)
    def _():
        step(proj_a, proj_b)

    @pl.when(s % 2 == 1)
    def _():
        step(proj_b, proj_a)


def _starts_sequence(blocks_per_seq):
    s = pl.program_id(0)
    return jnp.logical_or(s == 0, (s + blocks_per_seq - 1) % blocks_per_seq == 0)


def _lru_kernel(x_ref, perm_ref, permt_ref, w_ref, cw_ref, cb_ref, wai_ref, ba_ref, bi_ref, lam_ref, o_ref,
                proj_a, proj_b, halo_ref, h_ref, *, rows, blocks_per_seq):
    @pl.when(_starts_sequence(blocks_per_seq))
    def _():
        halo_ref[...] = jnp.zeros_like(halo_ref)
        h_ref[...] = jnp.zeros_like(h_ref)

    _pipelined_steps(functools.partial(_lru_step, x_ref, perm_ref, permt_ref, w_ref, cw_ref, cb_ref, wai_ref, ba_ref,
                                       bi_ref, lam_ref, o_ref, halo_ref, h_ref, rows), proj_a, proj_b)


def _lru_step(x_ref, perm_ref, permt_ref, w_ref, cw_ref, cb_ref, wai_ref, ba_ref, bi_ref, lam_ref, o_ref, halo_ref,
              h_ref, rows, produce_ref, proj):
    nblk = rows // PERM
    xp = _permute_rows(perm_ref[...], x_ref[...].astype(BF16))
    piece = 2 * D_LRU // LRU_BLOCKS
    cw = cw_ref[...]
    cb = cb_ref[...]
    prev = halo_ref[...]
    xcs = []
    for b in range(nblk):
        xc_b, prev = _perm_conv(proj[b * PERM:(b + 1) * PERM, :D_LRU], prev, cw, cb)
        xcs.append(xc_b)
    halo_ref[...] = prev
    xc = jnp.concatenate(xcs, axis=0)
    xcb = xc.astype(BF16)
    neg_c_sp = -LRU_C * _softplus(-lam_ref[...])

    ys = []
    for n in range(LRU_BLOCKS):
        lanes = slice(n * LRU_BLOCK, (n + 1) * LRU_BLOCK)
        ai = jnp.dot(xcb[:, lanes], wai_ref[n], preferred_element_type=F32)
        produce_ref[:, n * piece:(n + 1) * piece] = jnp.dot(xp, w_ref[:, n * piece:(n + 1) * piece],
                                                            preferred_element_type=F32)
        r = _sigmoid(ai[:, :LRU_BLOCK] + ba_ref[:, lanes])
        i = _sigmoid(ai[:, LRU_BLOCK:] + bi_ref[:, lanes])
        a = jnp.exp(neg_c_sp[:, lanes] * r)
        v = 1.0 - a * a
        u = jnp.where(v > 0.0, v * lax.rsqrt(v), 0.0) * (i * xc[:, lanes])
        carry = h_ref[:, lanes]
        hs = []
        for b in range(nblk):
            h_b, carry = _perm_scan(a[b * PERM:(b + 1) * PERM, :], u[b * PERM:(b + 1) * PERM, :], carry)
            hs.append(h_b)
        h_ref[:, lanes] = carry
        gate = proj[:, D_LRU + n * LRU_BLOCK:D_LRU + (n + 1) * LRU_BLOCK]
        ys.append((_gelu_tanh(gate) * jnp.concatenate(hs, axis=0)).astype(BF16))
    o_ref[...] = _permute_rows(permt_ref[...], jnp.concatenate(ys, axis=1))


def _lru_branch(x, layer, w_in, cw, cb, wai, ba, bi, lam):
    b, s, d = x.shape
    rows = min(LRU_TC, s)
    total = b * s // rows
    perms = _perm_matrices()
    params = (cw, cb, wai, ba, bi, lam)
    args = (x.reshape(total, rows, d),) + perms + (w_in,) + params
    y = pl.pallas_call(
        functools.partial(_lru_kernel, rows=rows, blocks_per_seq=s // rows),
        grid=(total + 1,),
        in_specs=[pl.BlockSpec((None, rows, d), lambda i: (jnp.minimum(i, total - 1), 0, 0))]
        + [_resident(a.shape) for a in perms] + [_layer(w_in, layer, (d, 2 * D_LRU))]
        + [_layer(a, layer) for a in params],
        out_specs=pl.BlockSpec((None, rows, D_LRU), lambda i: (jnp.maximum(i - 1, 0), 0, 0)),
        out_shape=jax.ShapeDtypeStruct((total, rows, D_LRU), BF16),
        scratch_shapes=[pltpu.VMEM((rows, 2 * D_LRU), F32), pltpu.VMEM((rows, 2 * D_LRU), F32),
                        pltpu.VMEM((HALO, D_LRU), F32), pltpu.VMEM((1, D_LRU), F32)],
        compiler_params=pltpu.CompilerParams(dimension_semantics=("arbitrary",), vmem_limit_bytes=VMEM_LIMIT),
        name="lru_branch",
    )(*args)
    return y.reshape(b, s, D_LRU)


def _head_expansion_matrix():
    e = np.zeros((2 * LANES, D_SSD), np.float32)
    for h in range(SSD_HEADS):
        for t in range(2):
            e[t * LANES + h, h * SSD_HEAD_DIM:(h + 1) * SSD_HEAD_DIM] = 1.0
    return jnp.asarray(e, BF16)


def _expand_heads(w, ehead):
    return jnp.dot(_split_bf16(w, 2), ehead, preferred_element_type=F32)


def _ssd_kernel(x_ref, perm_ref, permt_ref, w_ref, cw_ref, cb_ref, dtb_ref, alog_ref, dskip_ref, nw_ref, ehead_ref,
                o_ref, proj_a, proj_b, halo_ref, state_ref, *, rows, blocks_per_seq):
    @pl.when(_starts_sequence(blocks_per_seq))
    def _():
        halo_ref[...] = jnp.zeros_like(halo_ref)
        state_ref[...] = jnp.zeros_like(state_ref)

    _pipelined_steps(functools.partial(_ssd_step, x_ref, perm_ref, permt_ref, w_ref, cw_ref, cb_ref, dtb_ref, alog_ref,
                                       dskip_ref, nw_ref, ehead_ref, o_ref, halo_ref, state_ref, rows),
                     proj_a, proj_b)


def _ssd_step(x_ref, perm_ref, permt_ref, w_ref, cw_ref, cb_ref, dtb_ref, alog_ref, dskip_ref, nw_ref, ehead_ref, o_ref,
              halo_ref, state_ref, rows, produce_ref, proj):
    length = PERM
    xp = _permute_rows(perm_ref[...], x_ref[...].astype(BF16))
    n_pieces = (rows // length) * SSD_GROUPS
    mxu_cols = 2 * LANES
    n_tiles = pl.cdiv(w_ref.shape[1], mxu_cols)
    bounds = [min((i * n_tiles // n_pieces) * mxu_cols, w_ref.shape[1]) for i in range(n_pieces)] + [w_ref.shape[1]]
    dt = _softplus(proj[:, D_SSD + D_XBC:] + dtb_ref[...])
    da = dt * (-jnp.exp(alog_ref[...]))
    r_i = lax.broadcasted_iota(jnp.int32, (length, length), 0)
    c_i = lax.broadcasted_iota(jnp.int32, (length, length), 1)
    tok = lambda p: SEG * (p & (SUBLANES - 1)) + (p >> 3)
    causal = tok(r_i) >= tok(c_i)
    tri = jnp.where(causal, 1.0, 0.0).astype(BF16)
    lane = lax.broadcasted_iota(jnp.int32, (length, LANES), 1)
    first_head = lane < SSD_HEAD_DIM
    ehead = ehead_ref[...]
    cw = cw_ref[...]
    cb_conv = cb_ref[...]
    prev = halo_ref[...]

    for c in range(rows // length):
        rs = slice(c * length, (c + 1) * length)
        conv, prev = _perm_conv(proj[rs, D_SSD:D_SSD + D_XBC], prev, cw, cb_conv)
        xbc = _silu(conv)
        xs = xbc[:, :D_SSD]
        dt_c = dt[rs, :]
        cs3 = jnp.dot(tri, _split_bf16(da[rs, :], 3), preferred_element_type=F32)
        cs = cs3[:, :LANES] + cs3[:, LANES:2 * LANES] + cs3[:, 2 * LANES:]
        cs_last = cs[length - 1:length, :]
        w_state = dt_c * jnp.exp(cs_last - cs)
        w_out = jnp.exp(cs)
        cs_t = cs.T
        dt_t = dt_c.T
        w_state_e = _expand_heads(w_state, ehead)
        w_out_e = _expand_heads(w_out, ehead)

        y_cols = []
        for g in range(SSD_GROUPS):
            tiles = []
            b_g = xbc[:, D_SSD + g * SSD_STATE:D_SSD + (g + 1) * SSD_STATE]
            c_g = xbc[:, D_SSD + D_BC + g * SSD_STATE:D_SSD + D_BC + (g + 1) * SSD_STATE].astype(BF16)
            cb = lax.dot_general(c_g, b_g.astype(BF16), (((1,), (1,)), ((), ())), preferred_element_type=F32)
            gl = slice(g * D_GROUP, (g + 1) * D_GROUP)
            heads_per_group = SSD_HEADS // SSD_GROUPS
            for p in range(heads_per_group // 2):
                h0 = g * heads_per_group + 2 * p
                ms = []
                for h in (h0, h0 + 1):
                    seg = jnp.broadcast_to(cs[:, h:h + 1], (length, length)) - cs_t[h:h + 1, :]
                    dec = jnp.where(causal, jnp.exp(jnp.minimum(seg, 0.0)), 0.0)
                    ms.append((cb * dec * dt_t[h:h + 1, :]).astype(BF16))
                x_t = xs[:, h0 * SSD_HEAD_DIM:(h0 + 2) * SSD_HEAD_DIM]
                rhs = jnp.concatenate([jnp.where(first_head, x_t, 0.0), jnp.where(first_head, 0.0, x_t)],
                                      axis=0).astype(BF16)
                tiles.append(jnp.dot(jnp.concatenate(ms, axis=1), rhs, preferred_element_type=F32))
            st = state_ref[g]
            y_off = jnp.dot(c_g, st.astype(BF16), preferred_element_type=F32) * w_out_e[:, gl]
            xd = (xs[:, gl] * w_state_e[:, gl]).astype(BF16)
            s_new = jnp.dot(b_g.T.astype(BF16), xd, preferred_element_type=F32)
            state_ref[g] = st * w_out_e[length - 1:length, gl] + s_new
            y_cols.append(jnp.concatenate(tiles, axis=1) + y_off)
            lo, hi = bounds[c * SSD_GROUPS + g], bounds[c * SSD_GROUPS + g + 1]
            produce_ref[:, lo:hi] = jnp.dot(xp, w_ref[:, lo:hi], preferred_element_type=F32)

        y = jnp.concatenate(y_cols, axis=1) + dskip_ref[...] * xs
        y = y * _silu(proj[rs, :D_SSD])
        outs = []
        for g in range(SSD_GROUPS):
            yg = y[:, g * D_GROUP:(g + 1) * D_GROUP]
            outs.append(yg * lax.rsqrt(jnp.mean(yg * yg, axis=-1, keepdims=True) + EPS))
        y_out = (jnp.concatenate(outs, axis=1) * nw_ref[...]).astype(BF16)
        o_ref[rs, :] = _permute_rows(permt_ref[...], y_out)
    halo_ref[...] = prev


def _ssd_branch(x, layer, w, cw, cb, dtb, alog, dskip, nw):
    b, s, d = x.shape
    rows = min(SSD_TC, s)
    total = b * s // rows
    width = w.shape[2]
    perms = _perm_matrices()
    params = (w, cw, cb, dtb, alog, dskip, nw)
    ehead = _head_expansion_matrix()
    args = (x.reshape(total, rows, d),) + perms + params + (ehead,)
    y = pl.pallas_call(
        functools.partial(_ssd_kernel, rows=rows, blocks_per_seq=s // rows),
        grid=(total + 1,),
        in_specs=[pl.BlockSpec((None, rows, d), lambda i: (jnp.minimum(i, total - 1), 0, 0))]
        + [_resident(a.shape) for a in perms] + [_layer(a, layer) for a in params] + [_resident(ehead.shape)],
        out_specs=pl.BlockSpec((None, rows, D_SSD), lambda i: (jnp.maximum(i - 1, 0), 0, 0)),
        out_shape=jax.ShapeDtypeStruct((total, rows, D_SSD), BF16),
        scratch_shapes=[pltpu.VMEM((rows, width), F32), pltpu.VMEM((rows, width), F32),
                        pltpu.VMEM((HALO, D_XBC), F32), pltpu.VMEM((SSD_GROUPS, SSD_STATE, D_GROUP), F32)],
        compiler_params=pltpu.CompilerParams(dimension_semantics=("arbitrary",), vmem_limit_bytes=VMEM_LIMIT),
        name="ssd_branch",
    )(*args)
    return y.reshape(b, s, D_SSD)


def _mix_kernel(x_ref, wqg_ref, kv_ref, ylru_ref, yssd_ref, bg_ref, wl_ref, ws_ref, wx_ref, wo_ref, g_ref, b_ref,
                o_ref, *, alpha, sub):
    dot = functools.partial(jnp.dot, preferred_element_type=F32)
    blocks = [slice(i * sub, (i + 1) * sub) for i in range(x_ref.shape[0] // sub)]
    xf = [x_ref[r, :] for r in blocks]
    pq = [dot(x.astype(BF16), wqg_ref[...]) for x in xf]
    scores = []
    for p in pq:
        for h in range(XA_HEADS):
            q_h = p[:, h * XA_HEAD_DIM:(h + 1) * XA_HEAD_DIM].astype(BF16)
            k_h = kv_ref[:, h * XA_HEAD_DIM:(h + 1) * XA_HEAD_DIM]
            scores.append(lax.dot_general(q_h, k_h, (((1,), (1,)), ((), ())), preferred_element_type=F32)
                          * (XA_HEAD_DIM ** -0.5))
    br_lru = [dot(ylru_ref[r, :], wl_ref[...]) for r in blocks]
    br_ssd = [dot(yssd_ref[r, :], ws_ref[...]) for r in blocks]
    outs = []
    for sc in scores:
        e = jnp.exp(sc - jnp.max(sc, axis=-1, keepdims=True))
        probs = (e / jnp.sum(e, axis=-1, keepdims=True)).astype(BF16)
        h = len(outs) % XA_HEADS
        outs.append(dot(probs, kv_ref[:, D_XA + h * XA_HEAD_DIM:D_XA + (h + 1) * XA_HEAD_DIM]))
    br_xa = [dot(jnp.concatenate(outs[i * XA_HEADS:(i + 1) * XA_HEADS], axis=1).astype(BF16), wx_ref[...])
             for i in range(len(blocks))]
    merged = []
    for i, p in enumerate(pq):
        gates = _sigmoid(p[:, D_XA:] + bg_ref[...])
        merged.append((gates[:, :D_MODEL] * br_lru[i] + gates[:, D_MODEL:2 * D_MODEL] * br_ssd[i]
                       + gates[:, 2 * D_MODEL:] * br_xa[i]).astype(BF16))
    mix = [dot(m, wo_ref[...]) for m in merged]
    for i, r in enumerate(blocks):
        o_ref[r, :] = _layer_norm(alpha * xf[i] + mix[i], g_ref[...], b_ref[...])


def _mix_layer(x, layer, wqg, kv, ylru, yssd, bg, wl, ws, wx, wo, g, bb, alpha):
    b, s, d = x.shape
    rows = min(MIX_TM, s)
    m = kv.shape[1]
    tok = lambda width: pl.BlockSpec((None, rows, width), lambda i, j: (i, j, 0))
    return pl.pallas_call(
        functools.partial(_mix_kernel, alpha=alpha, sub=min(MIX_SUB, rows)),
        grid=(b, s // rows),
        in_specs=[tok(d), _layer(wqg, layer), pl.BlockSpec((None, m, kv.shape[2]), lambda i, j: (i, 0, 0)),
                  tok(D_LRU), tok(D_SSD)] + [_layer(a, layer) for a in (bg, wl, ws, wx, wo, g, bb)],
        out_specs=tok(d),
        out_shape=jax.ShapeDtypeStruct((b, s, d), F32),
        compiler_params=pltpu.CompilerParams(dimension_semantics=("arbitrary", "arbitrary"),
                                             vmem_limit_bytes=VMEM_LIMIT),
        name="mix_layer",
    )(x, wqg, kv, ylru, yssd, bg, wl, ws, wx, wo, g, bb)


def _ffn_kernel(x_ref, wi_ref, wd_ref, g_ref, b_ref, o_ref, *, alpha, sub):
    dot = functools.partial(jnp.dot, preferred_element_type=F32)
    blocks = [slice(i * sub, (i + 1) * sub) for i in range(x_ref.shape[0] // sub)]
    xf = [x_ref[r, :] for r in blocks]
    hid = [dot(x.astype(BF16), wi_ref[...]) for x in xf]
    down = [dot((_silu(h[:, :D_FF]) * h[:, D_FF:]).astype(BF16), wd_ref[...]) for h in hid]
    for i, r in enumerate(blocks):
        o_ref[r, :] = _layer_norm(alpha * xf[i] + down[i], g_ref[...], b_ref[...])


def _ffn_layer(x, layer, wi, wd, g, bb, alpha):
    b, s, d = x.shape
    rows = min(FFN_TM, s)
    tok = pl.BlockSpec((None, rows, d), lambda i, j: (i, j, 0))
    return pl.pallas_call(
        functools.partial(_ffn_kernel, alpha=alpha, sub=min(FFN_SUB, rows)),
        grid=(b, s // rows),
        in_specs=[tok] + [_layer(a, layer) for a in (wi, wd, g, bb)],
        out_specs=tok,
        out_shape=jax.ShapeDtypeStruct((b, s, d), F32),
        compiler_params=pltpu.CompilerParams(dimension_semantics=("arbitrary", "arbitrary"),
                                             vmem_limit_bytes=VMEM_LIMIT),
        name="ffn_layer",
    )(x, wi, wd, g, bb)


def kernel(x, mem, w_in, b_gate, lru_conv_w, lru_conv_b, lru_w_a, lru_b_a, lru_w_i, lru_b_i, lru_lambda, ssd_conv_w,
           ssd_conv_b, ssd_dt_bias, ssd_a_log, ssd_d, ssd_norm_w, mem_w_kv, w_br_lru, w_br_ssd, w_br_xa, w_out, ln1_g,
           ln1_b, ffn_w_in, ffn_w_down, ln2_g, ln2_b):
    depth = w_in.shape[0]
    alpha = (2 * depth) ** 0.25
    o_z = 2 * D_LRU
    o_xbc = o_z + D_SSD
    o_dt = o_xbc + D_XBC
    o_q = o_dt + SSD_HEADS

    w_in_bf = w_in.astype(BF16)
    w_ssd = w_in_bf[:, :, o_z:o_dt + DT_PAD]
    w_qg = w_in_bf[:, :, o_q:]
    w_ai = jnp.concatenate([lru_w_a, lru_w_i], axis=-1).astype(BF16)
    pad_heads = lambda a: jnp.pad(a, ((0, 0), (0, DT_PAD - SSD_HEADS)))[:, None, :]
    dtb = pad_heads(ssd_dt_bias)
    alog = pad_heads(ssd_a_log)
    dskip = jnp.repeat(ssd_d, SSD_HEAD_DIM, axis=-1)[:, None, :]
    row = lambda a: a[:, None, :]
    w_kv = mem_w_kv.astype(BF16)
    wl, ws, wx, wo = (w.astype(BF16) for w in (w_br_lru, w_br_ssd, w_br_xa, w_out))
    wi, wd = ffn_w_in.astype(BF16), ffn_w_down.astype(BF16)
    bg = b_gate.reshape(depth, 1, N_BRANCH * D_MODEL)
    lru_params = (lru_conv_w, row(lru_conv_b), w_ai, row(lru_b_a), row(lru_b_i), row(lru_lambda))
    ssd_params = (w_ssd, ssd_conv_w, row(ssd_conv_b), dtb, alog, dskip, row(ssd_norm_w))
    ln1 = (row(ln1_g), row(ln1_b))
    ln2 = (row(ln2_g), row(ln2_b))

    for l in range(depth):
        kv = _kv_proj(mem, w_kv, l)
        y_lru = _lru_branch(x, l, w_in_bf, *lru_params)
        y_ssd = _ssd_branch(x, l, *ssd_params)
        x = _mix_layer(x, l, w_qg, kv, y_lru, y_ssd, bg, wl, ws, wx, wo, *ln1, alpha)
        x = _ffn_layer(x, l, wi, wd, *ln2, alpha)
    return x
```

```python
import functools

import numpy as np
import jax
import jax.numpy as jnp
from jax import lax
from jax.experimental import pallas as pl
from jax.experimental.pallas import tpu as pltpu

F32 = jnp.float32
BF16 = jnp.bfloat16

D_MODEL = 1024
CONV_WIDTH = 4
D_LRU = D_MODEL
LRU_BLOCKS = 8
LRU_BLOCK = D_LRU // LRU_BLOCKS
LRU_C = 8.0
D_SSD = 2 * D_MODEL
SSD_HEAD_DIM = 64
SSD_HEADS = D_SSD // SSD_HEAD_DIM
SSD_GROUPS = 4
SSD_STATE = 128
D_BC = SSD_GROUPS * SSD_STATE
D_XBC = D_SSD + 2 * D_BC
D_GROUP = D_SSD // SSD_GROUPS
XA_HEADS = 4
XA_HEAD_DIM = 256
D_XA = XA_HEADS * XA_HEAD_DIM
N_BRANCH = 3
D_FF = ((8 * D_MODEL // 3 + 255) // 256) * 256
EPS = 1e-5

SUBLANES = 8
LANES = 128
PERM = 128
SEG = PERM // SUBLANES
HALO = (CONV_WIDTH - 1) * SUBLANES
DT_PAD = LANES

LRU_TC = 512
SSD_TC = 256
MIX_TM = 512
MIX_SUB = 256
FFN_TM = 512
FFN_SUB = 256
VMEM_LIMIT = 56 * 1024 * 1024


def _resident(shape):
    nd = len(shape)
    return pl.BlockSpec(shape, lambda *_: (0,) * nd, pipeline_mode=pl.Buffered(1))


def _layer(arr, layer, block=None):
    shape = tuple(arr.shape[1:] if block is None else block)
    return pl.BlockSpec((None,) + shape, lambda *_: (layer,) + (0,) * len(shape), pipeline_mode=pl.Buffered(1))


LOG2E = 1.4426950408889634
NON_CAUSAL_LOG2 = -1e30


def _sigmoid(x):
    return 1.0 / (1.0 + jnp.exp2(x * -LOG2E))


def _silu(x):
    return x * _sigmoid(x)


def _softplus(x):
    return jnp.maximum(x, 0.0) + jnp.log1p(jnp.exp(-jnp.abs(x)))


def _gelu_tanh(x):
    return 0.5 * x * (1.0 + jnp.tanh(0.7978845608028654 * (x + 0.044715 * (x * x * x))))


def _layer_norm(v, g, b):
    mu = jnp.mean(v, axis=-1, keepdims=True)
    c = v - mu
    var = jnp.mean(c * c, axis=-1, keepdims=True)
    return c * lax.rsqrt(var + EPS) * g + b


def _perm_matrices():
    p = np.zeros((PERM, PERM), np.float32)
    for pos in range(PERM):
        j, m = divmod(pos, SUBLANES)
        p[pos, SEG * m + j] = 1.0
    return jnp.asarray(p, BF16), jnp.asarray(p.T, BF16)


def _permute_rows(perm, v):
    blocks = [jnp.dot(perm, v[b * PERM:(b + 1) * PERM, :], preferred_element_type=F32).astype(BF16)
              for b in range(v.shape[0] // PERM)]
    return blocks[0] if len(blocks) == 1 else jnp.concatenate(blocks, axis=0)


def _perm_conv(cur, prev_rolled, cw, cb):
    c = cur.shape[1]
    taps = CONV_WIDTH - 1
    rolled = pltpu.roll(cur[PERM - HALO:, :].reshape(taps, SUBLANES, c), 1, 1)
    sub = lax.broadcasted_iota(jnp.int32, (taps, SUBLANES, c), 1)
    halo = jnp.where(sub == 0, prev_rolled.reshape(taps, SUBLANES, c), rolled).reshape(HALO, c)
    ext = jnp.concatenate([halo, cur], axis=0)
    y = cb + cw[taps:taps + 1, :] * cur
    for k in range(1, CONV_WIDTH):
        y = y + cw[taps - k:taps - k + 1, :] * ext[HALO - SUBLANES * k:HALO - SUBLANES * k + PERM, :]
    return y, rolled.reshape(HALO, c)


def _split_bf16(v, terms):
    parts = []
    rem = v
    for _ in range(terms):
        p = rem.astype(BF16)
        parts.append(p)
        rem = rem - p.astype(F32)
    return jnp.concatenate(parts, axis=1)


def _kv_kernel(mem_ref, w_ref, o_ref):
    o_ref[...] = jnp.dot(mem_ref[...].astype(BF16), w_ref[...], preferred_element_type=F32).astype(BF16)


def _kv_proj(mem, w_kv, layer):
    b, m, d = mem.shape
    n = w_kv.shape[2]
    return pl.pallas_call(
        _kv_kernel,
        grid=(b,),
        in_specs=[pl.BlockSpec((None, m, d), lambda i: (i, 0, 0)), _layer(w_kv, layer)],
        out_specs=pl.BlockSpec((None, m, n), lambda i: (i, 0, 0)),
        out_shape=jax.ShapeDtypeStruct((b, m, n), BF16),
        compiler_params=pltpu.CompilerParams(dimension_semantics=("arbitrary",), vmem_limit_bytes=VMEM_LIMIT),
        name="kv_proj",
    )(mem, w_kv)


def _perm_scan(a, u, h_in):
    width = a.shape[1]
    a3 = a.reshape(SEG, SUBLANES, width)
    u3 = u.reshape(SEG, SUBLANES, width)
    h_loc = [u3[0]]
    p_loc = [a3[0]]
    for j in range(1, SEG):
        h_loc.append(a3[j] * h_loc[-1] + u3[j])
        p_loc.append(a3[j] * p_loc[-1])
    ends_h, ends_p = h_loc[-1], p_loc[-1]
    carry = h_in
    starts = []
    for m in range(SUBLANES):
        starts.append(carry)
        carry = ends_h[m:m + 1, :] + ends_p[m:m + 1, :] * carry
    start = jnp.concatenate(starts, axis=0)
    h = jnp.stack([h_loc[j] + p_loc[j] * start for j in range(SEG)], axis=0)
    return h.reshape(PERM, width), carry


def _pipelined_steps(step, proj_a, proj_b):
    s = pl.program_id(0)

    @pl.when(s == 0)
    def _():
        proj_b[...] = jnp.zeros_like(proj_b)

    @pl.when(s % 2 == 0)
    def _():
        step(proj_a, proj_b)

    @pl.when(s % 2 == 1)
    def _():
        step(proj_b, proj_a)


def _starts_sequence(blocks_per_seq):
    s = pl.program_id(0)
    return jnp.logical_or(s == 0, (s + blocks_per_seq - 1) % blocks_per_seq == 0)


def _lru_kernel(x_ref, perm_ref, permt_ref, w_ref, cw_ref, cb_ref, wai_ref, ba_ref, bi_ref, lam_ref, o_ref,
                proj_a, proj_b, halo_ref, h_ref, *, rows, blocks_per_seq):
    @pl.when(_starts_sequence(blocks_per_seq))
    def _():
        halo_ref[...] = jnp.zeros_like(halo_ref)
        h_ref[...] = jnp.zeros_like(h_ref)

    _pipelined_steps(functools.partial(_lru_step, x_ref, perm_ref, permt_ref, w_ref, cw_ref, cb_ref, wai_ref, ba_ref,
                                       bi_ref, lam_ref, o_ref, halo_ref, h_ref, rows), proj_a, proj_b)


def _lru_step(x_ref, perm_ref, permt_ref, w_ref, cw_ref, cb_ref, wai_ref, ba_ref, bi_ref, lam_ref, o_ref, halo_ref,
              h_ref, rows, produce_ref, proj):
    nblk = rows // PERM
    xp = _permute_rows(perm_ref[...], x_ref[...].astype(BF16))
    piece = 2 * D_LRU // LRU_BLOCKS
    cw = cw_ref[...]
    cb = cb_ref[...]
    prev = halo_ref[...]
    xcs = []
    for b in range(nblk):
        xc_b, prev = _perm_conv(proj[b * PERM:(b + 1) * PERM, :D_LRU], prev, cw, cb)
        xcs.append(xc_b)
    halo_ref[...] = prev
    xc = jnp.concatenate(xcs, axis=0)
    xcb = xc.astype(BF16)
    neg_c_sp = (-LRU_C * LOG2E) * _softplus(-lam_ref[...])

    ys = []
    for n in range(LRU_BLOCKS):
        lanes = slice(n * LRU_BLOCK, (n + 1) * LRU_BLOCK)
        ai = jnp.dot(xcb[:, lanes], wai_ref[n], preferred_element_type=F32)
        produce_ref[:, n * piece:(n + 1) * piece] = jnp.dot(xp, w_ref[:, n * piece:(n + 1) * piece],
                                                            preferred_element_type=F32)
        r = _sigmoid(ai[:, :LRU_BLOCK] + ba_ref[:, lanes])
        i = _sigmoid(ai[:, LRU_BLOCK:] + bi_ref[:, lanes])
        a = jnp.exp2(neg_c_sp[:, lanes] * r)
        v = 1.0 - a * a
        u = jnp.where(v > 0.0, v * lax.rsqrt(v), 0.0) * (i * xc[:, lanes])
        carry = h_ref[:, lanes]
        hs = []
        for b in range(nblk):
            h_b, carry = _perm_scan(a[b * PERM:(b + 1) * PERM, :], u[b * PERM:(b + 1) * PERM, :], carry)
            hs.append(h_b)
        h_ref[:, lanes] = carry
        gate = proj[:, D_LRU + n * LRU_BLOCK:D_LRU + (n + 1) * LRU_BLOCK]
        ys.append((_gelu_tanh(gate) * jnp.concatenate(hs, axis=0)).astype(BF16))
    o_ref[...] = _permute_rows(permt_ref[...], jnp.concatenate(ys, axis=1))


def _lru_branch(x, layer, w_in, cw, cb, wai, ba, bi, lam):
    b, s, d = x.shape
    rows = min(LRU_TC, s)
    total = b * s // rows
    perms = _perm_matrices()
    params = (cw, cb, wai, ba, bi, lam)
    args = (x.reshape(total, rows, d),) + perms + (w_in,) + params
    y = pl.pallas_call(
        functools.partial(_lru_kernel, rows=rows, blocks_per_seq=s // rows),
        grid=(total + 1,),
        in_specs=[pl.BlockSpec((None, rows, d), lambda i: (jnp.minimum(i, total - 1), 0, 0))]
        + [_resident(a.shape) for a in perms] + [_layer(w_in, layer, (d, 2 * D_LRU))]
        + [_layer(a, layer) for a in params],
        out_specs=pl.BlockSpec((None, rows, D_LRU), lambda i: (jnp.maximum(i - 1, 0), 0, 0)),
        out_shape=jax.ShapeDtypeStruct((total, rows, D_LRU), BF16),
        scratch_shapes=[pltpu.VMEM((rows, 2 * D_LRU), F32), pltpu.VMEM((rows, 2 * D_LRU), F32),
                        pltpu.VMEM((HALO, D_LRU), F32), pltpu.VMEM((1, D_LRU), F32)],
        compiler_params=pltpu.CompilerParams(dimension_semantics=("arbitrary",), vmem_limit_bytes=VMEM_LIMIT),
        name="lru_branch",
    )(*args)
    return y.reshape(b, s, D_LRU)


def _head_expansion_matrix():
    e = np.zeros((2 * LANES, D_SSD), np.float32)
    for h in range(SSD_HEADS):
        for t in range(2):
            e[t * LANES + h, h * SSD_HEAD_DIM:(h + 1) * SSD_HEAD_DIM] = 1.0
    return jnp.asarray(e, BF16)


def _expand_heads(w, ehead):
    return jnp.dot(_split_bf16(w, 2), ehead, preferred_element_type=F32)


def _ssd_kernel(x_ref, perm_ref, permt_ref, w_ref, cw_ref, cb_ref, dtb_ref, alog_ref, dskip_ref, nw_ref, ehead_ref,
                o_ref, proj_a, proj_b, halo_ref, state_ref, *, rows, blocks_per_seq):
    @pl.when(_starts_sequence(blocks_per_seq))
    def _():
        halo_ref[...] = jnp.zeros_like(halo_ref)
        state_ref[...] = jnp.zeros_like(state_ref)

    _pipelined_steps(functools.partial(_ssd_step, x_ref, perm_ref, permt_ref, w_ref, cw_ref, cb_ref, dtb_ref, alog_ref,
                                       dskip_ref, nw_ref, ehead_ref, o_ref, halo_ref, state_ref, rows),
                     proj_a, proj_b)


def _ssd_step(x_ref, perm_ref, permt_ref, w_ref, cw_ref, cb_ref, dtb_ref, alog_ref, dskip_ref, nw_ref, ehead_ref, o_ref,
              halo_ref, state_ref, rows, produce_ref, proj):
    length = PERM
    xp = _permute_rows(perm_ref[...], x_ref[...].astype(BF16))
    n_pieces = (rows // length) * SSD_GROUPS
    mxu_cols = 2 * LANES
    n_tiles = pl.cdiv(w_ref.shape[1], mxu_cols)
    bounds = [min((i * n_tiles // n_pieces) * mxu_cols, w_ref.shape[1]) for i in range(n_pieces)] + [w_ref.shape[1]]
    dt = _softplus(proj[:, D_SSD + D_XBC:] + dtb_ref[...])
    da = dt * (-jnp.exp(alog_ref[...]))
    r_i = lax.broadcasted_iota(jnp.int32, (length, length), 0)
    c_i = lax.broadcasted_iota(jnp.int32, (length, length), 1)
    tok = lambda p: SEG * (p & (SUBLANES - 1)) + (p >> 3)
    causal = tok(r_i) >= tok(c_i)
    tri = jnp.where(causal, 1.0, 0.0).astype(BF16)
    neg_mask = jnp.where(causal, 0.0, NON_CAUSAL_LOG2)
    lane = lax.broadcasted_iota(jnp.int32, (length, LANES), 1)
    first_head = lane < SSD_HEAD_DIM
    ehead = ehead_ref[...]
    cw = cw_ref[...]
    cb_conv = cb_ref[...]
    prev = halo_ref[...]

    for c in range(rows // length):
        rs = slice(c * length, (c + 1) * length)
        conv, prev = _perm_conv(proj[rs, D_SSD:D_SSD + D_XBC], prev, cw, cb_conv)
        xbc = _silu(conv)
        xs = xbc[:, :D_SSD]
        dt_c = dt[rs, :]
        cs3 = jnp.dot(tri, _split_bf16(da[rs, :], 3), preferred_element_type=F32)
        b_t = [xbc[:, D_SSD + g * SSD_STATE:D_SSD + (g + 1) * SSD_STATE].T.astype(BF16) for g in range(SSD_GROUPS)]
        c_all = [xbc[:, D_SSD + D_BC + g * SSD_STATE:D_SSD + D_BC + (g + 1) * SSD_STATE].astype(BF16)
                 for g in range(SSD_GROUPS)]
        cb_all = [jnp.dot(c_all[g], b_t[g], preferred_element_type=F32) for g in range(SSD_GROUPS)]
        cs = cs3[:, :LANES] + cs3[:, LANES:2 * LANES] + cs3[:, 2 * LANES:]
        csl = cs * LOG2E
        w_state = dt_c * jnp.exp2(csl[length - 1:length, :] - csl)
        w_out = jnp.exp2(csl)
        src_t = (csl - jnp.log2(dt_c)).T
        w_state_e = _expand_heads(w_state, ehead)
        w_out_e = _expand_heads(w_out, ehead)

        y_cols = []
        for g in range(SSD_GROUPS):
            lo, hi = bounds[c * SSD_GROUPS + g], bounds[c * SSD_GROUPS + g + 1]
            produce_ref[:, lo:hi] = jnp.dot(xp, w_ref[:, lo:hi], preferred_element_type=F32)
            tiles = []
            c_g = c_all[g]
            cb = cb_all[g]
            gl = slice(g * D_GROUP, (g + 1) * D_GROUP)
            heads_per_group = SSD_HEADS // SSD_GROUPS
            for p in range(heads_per_group // 2):
                h0 = g * heads_per_group + 2 * p
                ms = []
                for h in (h0, h0 + 1):
                    seg = jnp.broadcast_to(csl[:, h:h + 1], (length, length)) - src_t[h:h + 1, :] + neg_mask
                    ms.append((cb * jnp.exp2(seg)).astype(BF16))
                x_t = xs[:, h0 * SSD_HEAD_DIM:(h0 + 2) * SSD_HEAD_DIM]
                rhs = jnp.concatenate([jnp.where(first_head, x_t, 0.0), jnp.where(first_head, 0.0, x_t)],
                                      axis=0).astype(BF16)
                tiles.append(jnp.dot(jnp.concatenate(ms, axis=1), rhs, preferred_element_type=F32))
            st = state_ref[g]
            y_off = jnp.dot(c_g, st.astype(BF16), preferred_element_type=F32) * w_out_e[:, gl]
            xd = (xs[:, gl] * w_state_e[:, gl]).astype(BF16)
            s_new = jnp.dot(b_t[g], xd, preferred_element_type=F32)
            state_ref[g] = st * w_out_e[length - 1:length, gl] + s_new
            y_cols.append(jnp.concatenate(tiles, axis=1) + y_off)

        y = jnp.concatenate(y_cols, axis=1) + dskip_ref[...] * xs
        y = y * _silu(proj[rs, :D_SSD])
        outs = []
        for g in range(SSD_GROUPS):
            yg = y[:, g * D_GROUP:(g + 1) * D_GROUP]
            outs.append(yg * lax.rsqrt(jnp.mean(yg * yg, axis=-1, keepdims=True) + EPS))
        y_out = (jnp.concatenate(outs, axis=1) * nw_ref[...]).astype(BF16)
        o_ref[rs, :] = _permute_rows(permt_ref[...], y_out)
    halo_ref[...] = prev


def _ssd_branch(x, layer, w, cw, cb, dtb, alog, dskip, nw):
    b, s, d = x.shape
    rows = min(SSD_TC, s)
    total = b * s // rows
    width = w.shape[2]
    perms = _perm_matrices()
    params = (w, cw, cb, dtb, alog, dskip, nw)
    ehead = _head_expansion_matrix()
    args = (x.reshape(total, rows, d),) + perms + params + (ehead,)
    y = pl.pallas_call(
        functools.partial(_ssd_kernel, rows=rows, blocks_per_seq=s // rows),
        grid=(total + 1,),
        in_specs=[pl.BlockSpec((None, rows, d), lambda i: (jnp.minimum(i, total - 1), 0, 0))]
        + [_resident(a.shape) for a in perms] + [_layer(a, layer) for a in params] + [_resident(ehead.shape)],
        out_specs=pl.BlockSpec((None, rows, D_SSD), lambda i: (jnp.maximum(i - 1, 0), 0, 0)),
        out_shape=jax.ShapeDtypeStruct((total, rows, D_SSD), BF16),
        scratch_shapes=[pltpu.VMEM((rows, width), F32), pltpu.VMEM((rows, width), F32),
                        pltpu.VMEM((HALO, D_XBC), F32), pltpu.VMEM((SSD_GROUPS, SSD_STATE, D_GROUP), F32)],
        compiler_params=pltpu.CompilerParams(dimension_semantics=("arbitrary",), vmem_limit_bytes=VMEM_LIMIT),
        name="ssd_branch",
    )(*args)
    return y.reshape(b, s, D_SSD)


def _mix_kernel(x_ref, wqg_ref, kv_ref, ylru_ref, yssd_ref, bg_ref, wl_ref, ws_ref, wx_ref, wo_ref, g_ref, b_ref,
                o_ref, *, alpha, sub):
    dot = functools.partial(jnp.dot, preferred_element_type=F32)
    blocks = [slice(i * sub, (i + 1) * sub) for i in range(x_ref.shape[0] // sub)]
    xf = [x_ref[r, :] for r in blocks]
    pq = [dot(x.astype(BF16), wqg_ref[...]) for x in xf]
    scores = []
    for p in pq:
        for h in range(XA_HEADS):
            q_h = p[:, h * XA_HEAD_DIM:(h + 1) * XA_HEAD_DIM].astype(BF16)
            k_h = kv_ref[:, h * XA_HEAD_DIM:(h + 1) * XA_HEAD_DIM]
            scores.append(lax.dot_general(q_h, k_h, (((1,), (1,)), ((), ())), preferred_element_type=F32))
    br_lru = [dot(ylru_ref[r, :], wl_ref[...]) for r in blocks]
    br_ssd = [dot(yssd_ref[r, :], ws_ref[...]) for r in blocks]
    outs = []
    for sc in scores:
        e = jnp.exp2((sc - jnp.max(sc, axis=-1, keepdims=True)) * (LOG2E * XA_HEAD_DIM ** -0.5))
        probs = (e / jnp.sum(e, axis=-1, keepdims=True)).astype(BF16)
        h = len(outs) % XA_HEADS
        outs.append(dot(probs, kv_ref[:, D_XA + h * XA_HEAD_DIM:D_XA + (h + 1) * XA_HEAD_DIM]))
    br_xa = [dot(jnp.concatenate(outs[i * XA_HEADS:(i + 1) * XA_HEADS], axis=1).astype(BF16), wx_ref[...])
             for i in range(len(blocks))]
    merged = []
    for i, p in enumerate(pq):
        gates = _sigmoid(p[:, D_XA:] + bg_ref[...])
        merged.append((gates[:, :D_MODEL] * br_lru[i] + gates[:, D_MODEL:2 * D_MODEL] * br_ssd[i]
                       + gates[:, 2 * D_MODEL:] * br_xa[i]).astype(BF16))
    mix = [dot(m, wo_ref[...]) for m in merged]
    for i, r in enumerate(blocks):
        o_ref[r, :] = _layer_norm(alpha * xf[i] + mix[i], g_ref[...], b_ref[...])


def _mix_layer(x, layer, wqg, kv, ylru, yssd, bg, wl, ws, wx, wo, g, bb, alpha):
    b, s, d = x.shape
    rows = min(MIX_TM, s)
    m = kv.shape[1]
    tok = lambda width: pl.BlockSpec((None, rows, width), lambda i, j: (i, j, 0))
    return pl.pallas_call(
        functools.partial(_mix_kernel, alpha=alpha, sub=min(MIX_SUB, rows)),
        grid=(b, s // rows),
        in_specs=[tok(d), _layer(wqg, layer), pl.BlockSpec((None, m, kv.shape[2]), lambda i, j: (i, 0, 0)),
                  tok(D_LRU), tok(D_SSD)] + [_layer(a, layer) for a in (bg, wl, ws, wx, wo, g, bb)],
        out_specs=tok(d),
        out_shape=jax.ShapeDtypeStruct((b, s, d), F32),
        compiler_params=pltpu.CompilerParams(dimension_semantics=("arbitrary", "arbitrary"),
                                             vmem_limit_bytes=VMEM_LIMIT),
        name="mix_layer",
    )(x, wqg, kv, ylru, yssd, bg, wl, ws, wx, wo, g, bb)


def _ffn_kernel(x_ref, wi_ref, wd_ref, g_ref, b_ref, o_ref, *, alpha, sub):
    dot = functools.partial(jnp.dot, preferred_element_type=F32)
    blocks = [slice(i * sub, (i + 1) * sub) for i in range(x_ref.shape[0] // sub)]
    xf = [x_ref[r, :] for r in blocks]
    hid = [dot(x.astype(BF16), wi_ref[...]) for x in xf]
    down = [dot((_silu(h[:, :D_FF]) * h[:, D_FF:]).astype(BF16), wd_ref[...]) for h in hid]
    for i, r in enumerate(blocks):
        o_ref[r, :] = _layer_norm(alpha * xf[i] + down[i], g_ref[...], b_ref[...])


def _ffn_layer(x, layer, wi, wd, g, bb, alpha):
    b, s, d = x.shape
    rows = min(FFN_TM, s)
    tok = pl.BlockSpec((None, rows, d), lambda i, j: (i, j, 0))
    return pl.pallas_call(
        functools.partial(_ffn_kernel, alpha=alpha, sub=min(FFN_SUB, rows)),
        grid=(b, s // rows),
        in_specs=[tok] + [_layer(a, layer) for a in (wi, wd, g, bb)],
        out_specs=tok,
        out_shape=jax.ShapeDtypeStruct((b, s, d), F32),
        compiler_params=pltpu.CompilerParams(dimension_semantics=("arbitrary", "arbitrary"),
                                             vmem_limit_bytes=VMEM_LIMIT),
        name="ffn_layer",
    )(x, wi, wd, g, bb)


def kernel(x, mem, w_in, b_gate, lru_conv_w, lru_conv_b, lru_w_a, lru_b_a, lru_w_i, lru_b_i, lru_lambda, ssd_conv_w,
           ssd_conv_b, ssd_dt_bias, ssd_a_log, ssd_d, ssd_norm_w, mem_w_kv, w_br_lru, w_br_ssd, w_br_xa, w_out, ln1_g,
           ln1_b, ffn_w_in, ffn_w_down, ln2_g, ln2_b):
    depth = w_in.shape[0]
    alpha = (2 * depth) ** 0.25
    o_z = 2 * D_LRU
    o_xbc = o_z + D_SSD
    o_dt = o_xbc + D_XBC
    o_q = o_dt + SSD_HEADS

    w_in_bf = w_in.astype(BF16)
    w_ssd = w_in_bf[:, :, o_z:o_dt + DT_PAD]
    w_qg = w_in_bf[:, :, o_q:]
    w_ai = jnp.concatenate([lru_w_a, lru_w_i], axis=-1).astype(BF16)
    pad_heads = lambda a: jnp.pad(a, ((0, 0), (0, DT_PAD - SSD_HEADS)))[:, None, :]
    dtb = pad_heads(ssd_dt_bias)
    alog = pad_heads(ssd_a_log)
    dskip = jnp.repeat(ssd_d, SSD_HEAD_DIM, axis=-1)[:, None, :]
    row = lambda a: a[:, None, :]
    w_kv = mem_w_kv.astype(BF16)
    wl, ws, wx, wo = (w.astype(BF16) for w in (w_br_lru, w_br_ssd, w_br_xa, w_out))
    wi, wd = ffn_w_in.astype(BF16), ffn_w_down.astype(BF16)
    bg = b_gate.reshape(depth, 1, N_BRANCH * D_MODEL)
    lru_params = (lru_conv_w, row(lru_conv_b), w_ai, row(lru_b_a), row(lru_b_i), row(lru_lambda))
    ssd_params = (w_ssd, ssd_conv_w, row(ssd_conv_b), dtb, alog, dskip, row(ssd_norm_w))
    ln1 = (row(ln1_g), row(ln1_b))
    ln2 = (row(ln2_g), row(ln2_b))

    for l in range(depth):
        kv = _kv_proj(mem, w_kv, l)
        y_lru = _lru_branch(x, l, w_in_bf, *lru_params)
        y_ssd = _ssd_branch(x, l, *ssd_params)
        x = _mix_layer(x, l, w_qg, kv, y_lru, y_ssd, bg, wl, ws, wx, wo, *ln1, alpha)
        x = _ffn_layer(x, l, wi, wd, *ln2, alpha)
    return x
```

```python
import functools

import numpy as np
import jax
import jax.numpy as jnp
from jax import lax
from jax.experimental import pallas as pl
from jax.experimental.pallas import tpu as pltpu

F32 = jnp.float32
BF16 = jnp.bfloat16

D_MODEL = 1024
CONV_WIDTH = 4
D_LRU = D_MODEL
LRU_BLOCKS = 8
LRU_BLOCK = D_LRU // LRU_BLOCKS
LRU_C = 8.0
D_SSD = 2 * D_MODEL
SSD_HEAD_DIM = 64
SSD_HEADS = D_SSD // SSD_HEAD_DIM
SSD_GROUPS = 4
SSD_STATE = 128
D_BC = SSD_GROUPS * SSD_STATE
D_XBC = D_SSD + 2 * D_BC
D_GROUP = D_SSD // SSD_GROUPS
XA_HEADS = 4
XA_HEAD_DIM = 256
D_XA = XA_HEADS * XA_HEAD_DIM
N_BRANCH = 3
D_FF = ((8 * D_MODEL // 3 + 255) // 256) * 256
EPS = 1e-5

SUBLANES = 8
LANES = 128
PERM = 128
SEG = PERM // SUBLANES
HALO = (CONV_WIDTH - 1) * SUBLANES
DT_PAD = LANES

LRU_TC = 512
SSD_TC = 512
MIX_TM = 512
MIX_SUB = 256
FFN_TM = 512
FFN_SUB = 256
VMEM_LIMIT = 56 * 1024 * 1024


def _resident(shape):
    nd = len(shape)
    return pl.BlockSpec(shape, lambda *_: (0,) * nd, pipeline_mode=pl.Buffered(1))


def _layer(arr, layer, block=None):
    shape = tuple(arr.shape[1:] if block is None else block)
    return pl.BlockSpec((None,) + shape, lambda *_: (layer,) + (0,) * len(shape), pipeline_mode=pl.Buffered(1))


LOG2E = 1.4426950408889634
NON_CAUSAL_LOG2 = -1e30


def _sigmoid(x):
    return 1.0 / (1.0 + jnp.exp2(x * -LOG2E))


def _silu(x):
    return x * _sigmoid(x)


def _softplus(x):
    return jnp.maximum(x, 0.0) + jnp.log1p(jnp.exp(-jnp.abs(x)))


def _gelu_tanh(x):
    return 0.5 * x * (1.0 + jnp.tanh(0.7978845608028654 * (x + 0.044715 * (x * x * x))))


def _layer_norm(v, g, b):
    mu = jnp.mean(v, axis=-1, keepdims=True)
    c = v - mu
    var = jnp.mean(c * c, axis=-1, keepdims=True)
    return c * lax.rsqrt(var + EPS) * g + b


def _perm_matrices():
    p = np.zeros((PERM, PERM), np.float32)
    for pos in range(PERM):
        j, m = divmod(pos, SUBLANES)
        p[pos, SEG * m + j] = 1.0
    return jnp.asarray(p, BF16), jnp.asarray(p.T, BF16)


def _permute_rows(perm, v):
    blocks = [jnp.dot(perm, v[b * PERM:(b + 1) * PERM, :], preferred_element_type=F32).astype(BF16)
              for b in range(v.shape[0] // PERM)]
    return blocks[0] if len(blocks) == 1 else jnp.concatenate(blocks, axis=0)


def _perm_conv(cur, prev_rolled, cw, cb):
    c = cur.shape[1]
    taps = CONV_WIDTH - 1
    rolled = pltpu.roll(cur[PERM - HALO:, :].reshape(taps, SUBLANES, c), 1, 1)
    sub = lax.broadcasted_iota(jnp.int32, (taps, SUBLANES, c), 1)
    halo = jnp.where(sub == 0, prev_rolled.reshape(taps, SUBLANES, c), rolled).reshape(HALO, c)
    ext = jnp.concatenate([halo, cur], axis=0)
    y = cb + cw[taps:taps + 1, :] * cur
    for k in range(1, CONV_WIDTH):
        y = y + cw[taps - k:taps - k + 1, :] * ext[HALO - SUBLANES * k:HALO - SUBLANES * k + PERM, :]
    return y, rolled.reshape(HALO, c)


def _split_bf16(v, terms):
    parts = []
    rem = v
    for _ in range(terms):
        p = rem.astype(BF16)
        parts.append(p)
        rem = rem - p.astype(F32)
    return jnp.concatenate(parts, axis=1)


def _kv_kernel(mem_ref, w_ref, o_ref):
    o_ref[...] = jnp.dot(mem_ref[...].astype(BF16), w_ref[...], preferred_element_type=F32).astype(BF16)


def _kv_proj(mem, w_kv, layer):
    b, m, d = mem.shape
    n = w_kv.shape[2]
    return pl.pallas_call(
        _kv_kernel,
        grid=(b,),
        in_specs=[pl.BlockSpec((None, m, d), lambda i: (i, 0, 0)), _layer(w_kv, layer)],
        out_specs=pl.BlockSpec((None, m, n), lambda i: (i, 0, 0)),
        out_shape=jax.ShapeDtypeStruct((b, m, n), BF16),
        compiler_params=pltpu.CompilerParams(dimension_semantics=("arbitrary",), vmem_limit_bytes=VMEM_LIMIT),
        name="kv_proj",
    )(mem, w_kv)


def _perm_scan(a, u, h_in):
    width = a.shape[1]
    a3 = a.reshape(SEG, SUBLANES, width)
    u3 = u.reshape(SEG, SUBLANES, width)
    h_loc = [u3[0]]
    p_loc = [a3[0]]
    for j in range(1, SEG):
        h_loc.append(a3[j] * h_loc[-1] + u3[j])
        p_loc.append(a3[j] * p_loc[-1])
    ends_h, ends_p = h_loc[-1], p_loc[-1]
    carry = h_in
    starts = []
    for m in range(SUBLANES):
        starts.append(carry)
        carry = ends_h[m:m + 1, :] + ends_p[m:m + 1, :] * carry
    start = jnp.concatenate(starts, axis=0)
    h = jnp.stack([h_loc[j] + p_loc[j] * start for j in range(SEG)], axis=0)
    return h.reshape(PERM, width), carry


def _pipelined_steps(step, proj_a, proj_b):
    s = pl.program_id(0)

    @pl.when(s == 0)
    def _():
        proj_b[...] = jnp.zeros_like(proj_b)

    @pl.when(s % 2 == 0)
    def _():
        step(proj_a, proj_b)

    @pl.when(s % 2 == 1)
    def _():
        step(proj_b, proj_a)


def _starts_sequence(blocks_per_seq):
    s = pl.program_id(0)
    return jnp.logical_or(s == 0, (s + blocks_per_seq - 1) % blocks_per_seq == 0)


def _lru_kernel(x_ref, perm_ref, permt_ref, w_ref, cw_ref, cb_ref, wai_ref, ba_ref, bi_ref, lam_ref, o_ref,
                proj_a, proj_b, halo_ref, h_ref, *, rows, blocks_per_seq):
    @pl.when(_starts_sequence(blocks_per_seq))
    def _():
        halo_ref[...] = jnp.zeros_like(halo_ref)
        h_ref[...] = jnp.zeros_like(h_ref)

    _pipelined_steps(functools.partial(_lru_step, x_ref, perm_ref, permt_ref, w_ref, cw_ref, cb_ref, wai_ref, ba_ref,
                                       bi_ref, lam_ref, o_ref, halo_ref, h_ref, rows), proj_a, proj_b)


def _lru_step(x_ref, perm_ref, permt_ref, w_ref, cw_ref, cb_ref, wai_ref, ba_ref, bi_ref, lam_ref, o_ref, halo_ref,
              h_ref, rows, produce_ref, proj):
    nblk = rows // PERM
    xp = _permute_rows(perm_ref[...], x_ref[...].astype(BF16))
    piece = 2 * D_LRU // LRU_BLOCKS
    cw = cw_ref[...]
    cb = cb_ref[...]
    prev = halo_ref[...]
    xcs = []
    for b in range(nblk):
        xc_b, prev = _perm_conv(proj[b * PERM:(b + 1) * PERM, :D_LRU], prev, cw, cb)
        xcs.append(xc_b)
    halo_ref[...] = prev
    xc = jnp.concatenate(xcs, axis=0)
    xcb = xc.astype(BF16)
    neg_c_sp = (-LRU_C * LOG2E) * _softplus(-lam_ref[...])

    ys = []
    for n in range(LRU_BLOCKS):
        lanes = slice(n * LRU_BLOCK, (n + 1) * LRU_BLOCK)
        ai = jnp.dot(xcb[:, lanes], wai_ref[n], preferred_element_type=F32)
        produce_ref[:, n * piece:(n + 1) * piece] = jnp.dot(xp, w_ref[:, n * piece:(n + 1) * piece],
                                                            preferred_element_type=F32)
        r = _sigmoid(ai[:, :LRU_BLOCK] + ba_ref[:, lanes])
        i = _sigmoid(ai[:, LRU_BLOCK:] + bi_ref[:, lanes])
        a = jnp.exp2(neg_c_sp[:, lanes] * r)
        v = 1.0 - a * a
        u = jnp.where(v > 0.0, v * lax.rsqrt(v), 0.0) * (i * xc[:, lanes])
        carry = h_ref[:, lanes]
        hs = []
        for b in range(nblk):
            h_b, carry = _perm_scan(a[b * PERM:(b + 1) * PERM, :], u[b * PERM:(b + 1) * PERM, :], carry)
            hs.append(h_b)
        h_ref[:, lanes] = carry
        gate = proj[:, D_LRU + n * LRU_BLOCK:D_LRU + (n + 1) * LRU_BLOCK]
        ys.append((_gelu_tanh(gate) * jnp.concatenate(hs, axis=0)).astype(BF16))
    o_ref[...] = _permute_rows(permt_ref[...], jnp.concatenate(ys, axis=1))


def _lru_branch(x, layer, w_in, cw, cb, wai, ba, bi, lam):
    b, s, d = x.shape
    rows = min(LRU_TC, s)
    total = b * s // rows
    perms = _perm_matrices()
    params = (cw, cb, wai, ba, bi, lam)
    args = (x.reshape(total, rows, d),) + perms + (w_in,) + params
    y = pl.pallas_call(
        functools.partial(_lru_kernel, rows=rows, blocks_per_seq=s // rows),
        grid=(total + 1,),
        in_specs=[pl.BlockSpec((None, rows, d), lambda i: (jnp.minimum(i, total - 1), 0, 0))]
        + [_resident(a.shape) for a in perms] + [_layer(w_in, layer, (d, 2 * D_LRU))]
        + [_layer(a, layer) for a in params],
        out_specs=pl.BlockSpec((None, rows, D_LRU), lambda i: (jnp.maximum(i - 1, 0), 0, 0)),
        out_shape=jax.ShapeDtypeStruct((total, rows, D_LRU), BF16),
        scratch_shapes=[pltpu.VMEM((rows, 2 * D_LRU), F32), pltpu.VMEM((rows, 2 * D_LRU), F32),
                        pltpu.VMEM((HALO, D_LRU), F32), pltpu.VMEM((1, D_LRU), F32)],
        compiler_params=pltpu.CompilerParams(dimension_semantics=("arbitrary",), vmem_limit_bytes=VMEM_LIMIT),
        name="lru_branch",
    )(*args)
    return y.reshape(b, s, D_LRU)


def _head_expansion_matrix():
    e = np.zeros((2 * LANES, D_SSD), np.float32)
    for h in range(SSD_HEADS):
        for t in range(2):
            e[t * LANES + h, h * SSD_HEAD_DIM:(h + 1) * SSD_HEAD_DIM] = 1.0
    return jnp.asarray(e, BF16)


def _expand_heads(w, ehead):
    return jnp.dot(_split_bf16(w, 2), ehead, preferred_element_type=F32)


def _ssd_kernel(x_ref, perm_ref, permt_ref, w_ref, cw_ref, cb_ref, dtb_ref, alog_ref, dskip_ref, nw_ref, ehead_ref,
                o_ref, proj_a, proj_b, halo_ref, state_ref, *, rows, blocks_per_seq):
    @pl.when(_starts_sequence(blocks_per_seq))
    def _():
        halo_ref[...] = jnp.zeros_like(halo_ref)
        state_ref[...] = jnp.zeros_like(state_ref)

    _pipelined_steps(functools.partial(_ssd_step, x_ref, perm_ref, permt_ref, w_ref, cw_ref, cb_ref, dtb_ref, alog_ref,
                                       dskip_ref, nw_ref, ehead_ref, o_ref, halo_ref, state_ref, rows),
                     proj_a, proj_b)


def _ssd_step(x_ref, perm_ref, permt_ref, w_ref, cw_ref, cb_ref, dtb_ref, alog_ref, dskip_ref, nw_ref, ehead_ref, o_ref,
              halo_ref, state_ref, rows, produce_ref, proj):
    length = PERM
    xp = _permute_rows(perm_ref[...], x_ref[...].astype(BF16))
    n_pieces = (rows // length) * SSD_GROUPS
    mxu_cols = 2 * LANES
    n_tiles = pl.cdiv(w_ref.shape[1], mxu_cols)
    bounds = [min((i * n_tiles // n_pieces) * mxu_cols, w_ref.shape[1]) for i in range(n_pieces)] + [w_ref.shape[1]]
    dt = _softplus(proj[:, D_SSD + D_XBC:] + dtb_ref[...])
    da = dt * (-jnp.exp(alog_ref[...]))
    r_i = lax.broadcasted_iota(jnp.int32, (length, length), 0)
    c_i = lax.broadcasted_iota(jnp.int32, (length, length), 1)
    tok = lambda p: SEG * (p & (SUBLANES - 1)) + (p >> 3)
    causal = tok(r_i) >= tok(c_i)
    tri = jnp.where(causal, 1.0, 0.0).astype(BF16)
    neg_mask = jnp.where(causal, 0.0, NON_CAUSAL_LOG2)
    lane = lax.broadcasted_iota(jnp.int32, (length, LANES), 1)
    first_head = lane < SSD_HEAD_DIM
    ehead = ehead_ref[...]
    cw = cw_ref[...]
    cb_conv = cb_ref[...]
    prev = halo_ref[...]

    for c in range(rows // length):
        rs = slice(c * length, (c + 1) * length)
        conv, prev = _perm_conv(proj[rs, D_SSD:D_SSD + D_XBC], prev, cw, cb_conv)
        xbc = _silu(conv)
        xs = xbc[:, :D_SSD]
        dt_c = dt[rs, :]
        cs3 = jnp.dot(tri, _split_bf16(da[rs, :], 3), preferred_element_type=F32)
        b_t = [xbc[:, D_SSD + g * SSD_STATE:D_SSD + (g + 1) * SSD_STATE].T.astype(BF16) for g in range(SSD_GROUPS)]
        c_all = [xbc[:, D_SSD + D_BC + g * SSD_STATE:D_SSD + D_BC + (g + 1) * SSD_STATE].astype(BF16)
                 for g in range(SSD_GROUPS)]
        cb_all = [jnp.dot(c_all[g], b_t[g], preferred_element_type=F32) for g in range(SSD_GROUPS)]
        cs = cs3[:, :LANES] + cs3[:, LANES:2 * LANES] + cs3[:, 2 * LANES:]
        csl = cs * LOG2E
        w_state = dt_c * jnp.exp2(csl[length - 1:length, :] - csl)
        w_out = jnp.exp2(csl)
        src_t = (csl - jnp.log2(dt_c)).T
        w_state_e = _expand_heads(w_state, ehead)
        w_out_e = _expand_heads(w_out, ehead)

        y_cols = []
        for g in range(SSD_GROUPS):
            lo, hi = bounds[c * SSD_GROUPS + g], bounds[c * SSD_GROUPS + g + 1]
            produce_ref[:, lo:hi] = jnp.dot(xp, w_ref[:, lo:hi], preferred_element_type=F32)
            tiles = []
            c_g = c_all[g]
            cb = cb_all[g]
            gl = slice(g * D_GROUP, (g + 1) * D_GROUP)
            heads_per_group = SSD_HEADS // SSD_GROUPS
            for p in range(heads_per_group // 2):
                h0 = g * heads_per_group + 2 * p
                ms = []
                for h in (h0, h0 + 1):
                    seg = jnp.broadcast_to(csl[:, h:h + 1], (length, length)) - src_t[h:h + 1, :] + neg_mask
                    ms.append((cb * jnp.exp2(seg)).astype(BF16))
                x_t = xs[:, h0 * SSD_HEAD_DIM:(h0 + 2) * SSD_HEAD_DIM]
                rhs = jnp.concatenate([jnp.where(first_head, x_t, 0.0), jnp.where(first_head, 0.0, x_t)],
                                      axis=0).astype(BF16)
                tiles.append(jnp.dot(jnp.concatenate(ms, axis=1), rhs, preferred_element_type=F32))
            st = state_ref[g]
            y_off = jnp.dot(c_g, st.astype(BF16), preferred_element_type=F32) * w_out_e[:, gl]
            xd = (xs[:, gl] * w_state_e[:, gl]).astype(BF16)
            s_new = jnp.dot(b_t[g], xd, preferred_element_type=F32)
            state_ref[g] = st * w_out_e[length - 1:length, gl] + s_new
            y_cols.append(jnp.concatenate(tiles, axis=1) + y_off)

        y = jnp.concatenate(y_cols, axis=1) + dskip_ref[...] * xs
        y = y * _silu(proj[rs, :D_SSD])
        outs = []
        for g in range(SSD_GROUPS):
            yg = y[:, g * D_GROUP:(g + 1) * D_GROUP]
            outs.append(yg * lax.rsqrt(jnp.mean(yg * yg, axis=-1, keepdims=True) + EPS))
        y_out = (jnp.concatenate(outs, axis=1) * nw_ref[...]).astype(BF16)
        o_ref[rs, :] = _permute_rows(permt_ref[...], y_out)
    halo_ref[...] = prev


def _ssd_branch(x, layer, w, cw, cb, dtb, alog, dskip, nw):
    b, s, d = x.shape
    rows = min(SSD_TC, s)
    total = b * s // rows
    width = w.shape[2]
    perms = _perm_matrices()
    params = (w, cw, cb, dtb, alog, dskip, nw)
    ehead = _head_expansion_matrix()
    args = (x.reshape(total, rows, d),) + perms + params + (ehead,)
    y = pl.pallas_call(
        functools.partial(_ssd_kernel, rows=rows, blocks_per_seq=s // rows),
        grid=(total + 1,),
        in_specs=[pl.BlockSpec((None, rows, d), lambda i: (jnp.minimum(i, total - 1), 0, 0))]
        + [_resident(a.shape) for a in perms] + [_layer(a, layer) for a in params] + [_resident(ehead.shape)],
        out_specs=pl.BlockSpec((None, rows, D_SSD), lambda i: (jnp.maximum(i - 1, 0), 0, 0)),
        out_shape=jax.ShapeDtypeStruct((total, rows, D_SSD), BF16),
        scratch_shapes=[pltpu.VMEM((rows, width), F32), pltpu.VMEM((rows, width), F32),
                        pltpu.VMEM((HALO, D_XBC), F32), pltpu.VMEM((SSD_GROUPS, SSD_STATE, D_GROUP), F32)],
        compiler_params=pltpu.CompilerParams(dimension_semantics=("arbitrary",), vmem_limit_bytes=VMEM_LIMIT),
        name="ssd_branch",
    )(*args)
    return y.reshape(b, s, D_SSD)


def _mix_kernel(x_ref, wqg_ref, kv_ref, ylru_ref, yssd_ref, bg_ref, wl_ref, ws_ref, wx_ref, wo_ref, g_ref, b_ref,
                o_ref, *, alpha, sub):
    dot = functools.partial(jnp.dot, preferred_element_type=F32)
    blocks = [slice(i * sub, (i + 1) * sub) for i in range(x_ref.shape[0] // sub)]
    xf = [x_ref[r, :] for r in blocks]
    pq = [dot(x.astype(BF16), wqg_ref[...]) for x in xf]
    scores = []
    for p in pq:
        for h in range(XA_HEADS):
            q_h = p[:, h * XA_HEAD_DIM:(h + 1) * XA_HEAD_DIM].astype(BF16)
            k_h = kv_ref[:, h * XA_HEAD_DIM:(h + 1) * XA_HEAD_DIM]
            scores.append(lax.dot_general(q_h, k_h, (((1,), (1,)), ((), ())), preferred_element_type=F32))
    br_lru = [dot(ylru_ref[r, :], wl_ref[...]) for r in blocks]
    br_ssd = [dot(yssd_ref[r, :], ws_ref[...]) for r in blocks]
    outs = []
    for sc in scores:
        e = jnp.exp2((sc - jnp.max(sc, axis=-1, keepdims=True)) * (LOG2E * XA_HEAD_DIM ** -0.5))
        probs = (e / jnp.sum(e, axis=-1, keepdims=True)).astype(BF16)
        h = len(outs) % XA_HEADS
        outs.append(dot(probs, kv_ref[:, D_XA + h * XA_HEAD_DIM:D_XA + (h + 1) * XA_HEAD_DIM]))
    br_xa = [dot(jnp.concatenate(outs[i * XA_HEADS:(i + 1) * XA_HEADS], axis=1).astype(BF16), wx_ref[...])
             for i in range(len(blocks))]
    merged = []
    for i, p in enumerate(pq):
        gates = _sigmoid(p[:, D_XA:] + bg_ref[...])
        merged.append((gates[:, :D_MODEL] * br_lru[i] + gates[:, D_MODEL:2 * D_MODEL] * br_ssd[i]
                       + gates[:, 2 * D_MODEL:] * br_xa[i]).astype(BF16))
    mix = [dot(m, wo_ref[...]) for m in merged]
    for i, r in enumerate(blocks):
        o_ref[r, :] = _layer_norm(alpha * xf[i] + mix[i], g_ref[...], b_ref[...])


def _mix_layer(x, layer, wqg, kv, ylru, yssd, bg, wl, ws, wx, wo, g, bb, alpha):
    b, s, d = x.shape
    rows = min(MIX_TM, s)
    m = kv.shape[1]
    tok = lambda width: pl.BlockSpec((None, rows, width), lambda i, j: (i, j, 0))
    return pl.pallas_call(
        functools.partial(_mix_kernel, alpha=alpha, sub=min(MIX_SUB, rows)),
        grid=(b, s // rows),
        in_specs=[tok(d), _layer(wqg, layer), pl.BlockSpec((None, m, kv.shape[2]), lambda i, j: (i, 0, 0)),
                  tok(D_LRU), tok(D_SSD)] + [_layer(a, layer) for a in (bg, wl, ws, wx, wo, g, bb)],
        out_specs=tok(d),
        out_shape=jax.ShapeDtypeStruct((b, s, d), F32),
        compiler_params=pltpu.CompilerParams(dimension_semantics=("arbitrary", "arbitrary"),
                                             vmem_limit_bytes=VMEM_LIMIT),
        name="mix_layer",
    )(x, wqg, kv, ylru, yssd, bg, wl, ws, wx, wo, g, bb)


def _ffn_kernel(x_ref, wi_ref, wd_ref, g_ref, b_ref, o_ref, *, alpha, sub):
    dot = functools.partial(jnp.dot, preferred_element_type=F32)
    blocks = [slice(i * sub, (i + 1) * sub) for i in range(x_ref.shape[0] // sub)]
    xf = [x_ref[r, :] for r in blocks]
    hid = [dot(x.astype(BF16), wi_ref[...]) for x in xf]
    down = [dot((_silu(h[:, :D_FF]) * h[:, D_FF:]).astype(BF16), wd_ref[...]) for h in hid]
    for i, r in enumerate(blocks):
        o_ref[r, :] = _layer_norm(alpha * xf[i] + down[i], g_ref[...], b_ref[...])


def _ffn_layer(x, layer, wi, wd, g, bb, alpha):
    b, s, d = x.shape
    rows = min(FFN_TM, s)
    tok = pl.BlockSpec((None, rows, d), lambda i, j: (i, j, 0))
    return pl.pallas_call(
        functools.partial(_ffn_kernel, alpha=alpha, sub=min(FFN_SUB, rows)),
        grid=(b, s // rows),
        in_specs=[tok] + [_layer(a, layer) for a in (wi, wd, g, bb)],
        out_specs=tok,
        out_shape=jax.ShapeDtypeStruct((b, s, d), F32),
        compiler_params=pltpu.CompilerParams(dimension_semantics=("arbitrary", "arbitrary"),
                                             vmem_limit_bytes=VMEM_LIMIT),
        name="ffn_layer",
    )(x, wi, wd, g, bb)


def kernel(x, mem, w_in, b_gate, lru_conv_w, lru_conv_b, lru_w_a, lru_b_a, lru_w_i, lru_b_i, lru_lambda, ssd_conv_w,
           ssd_conv_b, ssd_dt_bias, ssd_a_log, ssd_d, ssd_norm_w, mem_w_kv, w_br_lru, w_br_ssd, w_br_xa, w_out, ln1_g,
           ln1_b, ffn_w_in, ffn_w_down, ln2_g, ln2_b):
    depth = w_in.shape[0]
    alpha = (2 * depth) ** 0.25
    o_z = 2 * D_LRU
    o_xbc = o_z + D_SSD
    o_dt = o_xbc + D_XBC
    o_q = o_dt + SSD_HEADS

    w_lru = w_in[:, :, :o_z].astype(BF16)
    w_ssd = w_in[:, :, o_z:o_dt + DT_PAD].astype(BF16)
    w_qg = w_in[:, :, o_q:].astype(BF16)
    w_ai = jnp.concatenate([lru_w_a, lru_w_i], axis=-1).astype(BF16)
    pad_heads = lambda a: jnp.pad(a, ((0, 0), (0, DT_PAD - SSD_HEADS)))[:, None, :]
    dtb = pad_heads(ssd_dt_bias)
    alog = pad_heads(ssd_a_log)
    dskip = jnp.repeat(ssd_d, SSD_HEAD_DIM, axis=-1)[:, None, :]
    row = lambda a: a[:, None, :]
    w_kv = mem_w_kv.astype(BF16)
    wl, ws, wx, wo = (w.astype(BF16) for w in (w_br_lru, w_br_ssd, w_br_xa, w_out))
    wi, wd = ffn_w_in.astype(BF16), ffn_w_down.astype(BF16)
    bg = b_gate.reshape(depth, 1, N_BRANCH * D_MODEL)
    lru_params = (lru_conv_w, row(lru_conv_b), w_ai, row(lru_b_a), row(lru_b_i), row(lru_lambda))
    ssd_params = (w_ssd, ssd_conv_w, row(ssd_conv_b), dtb, alog, dskip, row(ssd_norm_w))
    ln1 = (row(ln1_g), row(ln1_b))
    ln2 = (row(ln2_g), row(ln2_b))

    for l in range(depth):
        kv = _kv_proj(mem, w_kv, l)
        y_lru = _lru_branch(x, l, w_lru, *lru_params)
        y_ssd = _ssd_branch(x, l, *ssd_params)
        x = _mix_layer(x, l, w_qg, kv, y_lru, y_ssd, bg, wl, ws, wx, wo, *ln1, alpha)
        x = _ffn_layer(x, l, wi, wd, *ln2, alpha)
    return x
```

```python
import functools

import numpy as np
import jax
import jax.numpy as jnp
from jax import lax
from jax.experimental import pallas as pl
from jax.experimental.pallas import tpu as pltpu

F32 = jnp.float32
BF16 = jnp.bfloat16

D_MODEL = 1024
CONV_WIDTH = 4
D_LRU = D_MODEL
LRU_BLOCKS = 8
LRU_BLOCK = D_LRU // LRU_BLOCKS
LRU_C = 8.0
D_SSD = 2 * D_MODEL
SSD_HEAD_DIM = 64
SSD_HEADS = D_SSD // SSD_HEAD_DIM
SSD_GROUPS = 4
SSD_STATE = 128
D_BC = SSD_GROUPS * SSD_STATE
D_XBC = D_SSD + 2 * D_BC
D_GROUP = D_SSD // SSD_GROUPS
XA_HEADS = 4
XA_HEAD_DIM = 256
D_XA = XA_HEADS * XA_HEAD_DIM
N_BRANCH = 3
D_FF = ((8 * D_MODEL // 3 + 255) // 256) * 256
EPS = 1e-5

SUBLANES = 8
LANES = 128
PERM = 128
SEG = PERM // SUBLANES
HALO = (CONV_WIDTH - 1) * SUBLANES
DT_PAD = LANES

LRU_TC = 512
SSD_TC = 256
MIX_TM = 512
MIX_SUB = 256
FFN_TM = 1024
FFN_SUB = 256
VMEM_LIMIT = 56 * 1024 * 1024


def _resident(shape):
    nd = len(shape)
    return pl.BlockSpec(shape, lambda *_: (0,) * nd, pipeline_mode=pl.Buffered(1))


def _layer(arr, layer, block=None):
    shape = tuple(arr.shape[1:] if block is None else block)
    return pl.BlockSpec((None,) + shape, lambda *_: (layer,) + (0,) * len(shape), pipeline_mode=pl.Buffered(1))


LOG2E = 1.4426950408889634
NON_CAUSAL_LOG2 = -1e30


def _sigmoid(x):
    return 1.0 / (1.0 + jnp.exp2(x * -LOG2E))


def _silu(x):
    return x * _sigmoid(x)


def _softplus(x):
    return jnp.maximum(x, 0.0) + jnp.log1p(jnp.exp(-jnp.abs(x)))


def _gelu_tanh(x):
    return 0.5 * x * (1.0 + jnp.tanh(0.7978845608028654 * (x + 0.044715 * (x * x * x))))


def _layer_norm(v, g, b):
    mu = jnp.mean(v, axis=-1, keepdims=True)
    c = v - mu
    var = jnp.mean(c * c, axis=-1, keepdims=True)
    return c * lax.rsqrt(var + EPS) * g + b


def _perm_matrices():
    p = np.zeros((PERM, PERM), np.float32)
    for pos in range(PERM):
        j, m = divmod(pos, SUBLANES)
        p[pos, SEG * m + j] = 1.0
    return jnp.asarray(p, BF16), jnp.asarray(p.T, BF16)


def _permute_rows(perm, v):
    blocks = [jnp.dot(perm, v[b * PERM:(b + 1) * PERM, :], preferred_element_type=F32).astype(BF16)
              for b in range(v.shape[0] // PERM)]
    return blocks[0] if len(blocks) == 1 else jnp.concatenate(blocks, axis=0)


def _perm_conv(cur, prev_rolled, cw, cb):
    c = cur.shape[1]
    taps = CONV_WIDTH - 1
    rolled = pltpu.roll(cur[PERM - HALO:, :].reshape(taps, SUBLANES, c), 1, 1)
    sub = lax.broadcasted_iota(jnp.int32, (taps, SUBLANES, c), 1)
    halo = jnp.where(sub == 0, prev_rolled.reshape(taps, SUBLANES, c), rolled).reshape(HALO, c)
    ext = jnp.concatenate([halo, cur], axis=0)
    y = cb + cw[taps:taps + 1, :] * cur
    for k in range(1, CONV_WIDTH):
        y = y + cw[taps - k:taps - k + 1, :] * ext[HALO - SUBLANES * k:HALO - SUBLANES * k + PERM, :]
    return y, rolled.reshape(HALO, c)


def _split_bf16(v, terms):
    parts = []
    rem = v
    for _ in range(terms):
        p = rem.astype(BF16)
        parts.append(p)
        rem = rem - p.astype(F32)
    return jnp.concatenate(parts, axis=1)


def _kv_kernel(mem_ref, w_ref, o_ref):
    o_ref[...] = jnp.dot(mem_ref[...].astype(BF16), w_ref[...], preferred_element_type=F32).astype(BF16)


def _kv_proj(mem, w_kv, layer):
    b, m, d = mem.shape
    n = w_kv.shape[2]
    return pl.pallas_call(
        _kv_kernel,
        grid=(b,),
        in_specs=[pl.BlockSpec((None, m, d), lambda i: (i, 0, 0)), _layer(w_kv, layer)],
        out_specs=pl.BlockSpec((None, m, n), lambda i: (i, 0, 0)),
        out_shape=jax.ShapeDtypeStruct((b, m, n), BF16),
        compiler_params=pltpu.CompilerParams(dimension_semantics=("arbitrary",), vmem_limit_bytes=VMEM_LIMIT),
        name="kv_proj",
    )(mem, w_kv)


def _perm_cumsum(v):
    width = v.shape[1]
    v3 = v.reshape(SEG, SUBLANES, width)
    loc = [v3[0]]
    for j in range(1, SEG):
        loc.append(loc[-1] + v3[j])
    ends = loc[-1]
    carry = jnp.zeros((1, width), F32)
    starts = []
    for m in range(SUBLANES):
        starts.append(carry)
        carry = carry + ends[m:m + 1, :]
    start = jnp.concatenate(starts, axis=0)
    return jnp.stack([s + start for s in loc], axis=0).reshape(PERM, width)


def _perm_scan(a, u, h_in):
    width = a.shape[1]
    a3 = a.reshape(SEG, SUBLANES, width)
    u3 = u.reshape(SEG, SUBLANES, width)
    h_loc = [u3[0]]
    p_loc = [a3[0]]
    for j in range(1, SEG):
        h_loc.append(a3[j] * h_loc[-1] + u3[j])
        p_loc.append(a3[j] * p_loc[-1])
    ends_h, ends_p = h_loc[-1], p_loc[-1]
    carry = h_in
    starts = []
    for m in range(SUBLANES):
        starts.append(carry)
        carry = ends_h[m:m + 1, :] + ends_p[m:m + 1, :] * carry
    start = jnp.concatenate(starts, axis=0)
    h = jnp.stack([h_loc[j] + p_loc[j] * start for j in range(SEG)], axis=0)
    return h.reshape(PERM, width), carry


def _pipelined_steps(step, proj_a, proj_b):
    s = pl.program_id(0)

    @pl.when(s == 0)
    def _():
        proj_b[...] = jnp.zeros_like(proj_b)

    @pl.when(s % 2 == 0)
    def _():
        step(proj_a, proj_b)

    @pl.when(s % 2 == 1)
    def _():
        step(proj_b, proj_a)


def _starts_sequence(blocks_per_seq):
    s = pl.program_id(0)
    return jnp.logical_or(s == 0, (s + blocks_per_seq - 1) % blocks_per_seq == 0)


def _lru_kernel(x_ref, perm_ref, permt_ref, w_ref, cw_ref, cb_ref, wai_ref, ba_ref, bi_ref, lam_ref, o_ref,
                proj_a, proj_b, halo_ref, h_ref, *, rows, blocks_per_seq):
    @pl.when(_starts_sequence(blocks_per_seq))
    def _():
        halo_ref[...] = jnp.zeros_like(halo_ref)
        h_ref[...] = jnp.zeros_like(h_ref)

    _pipelined_steps(functools.partial(_lru_step, x_ref, perm_ref, permt_ref, w_ref, cw_ref, cb_ref, wai_ref, ba_ref,
                                       bi_ref, lam_ref, o_ref, halo_ref, h_ref, rows), proj_a, proj_b)


def _lru_step(x_ref, perm_ref, permt_ref, w_ref, cw_ref, cb_ref, wai_ref, ba_ref, bi_ref, lam_ref, o_ref, halo_ref,
              h_ref, rows, produce_ref, proj):
    nblk = rows // PERM
    xp = _permute_rows(perm_ref[...], x_ref[...].astype(BF16))
    piece = 2 * D_LRU // LRU_BLOCKS
    cw = cw_ref[...]
    cb = cb_ref[...]
    prev = halo_ref[...]
    xcs = []
    for b in range(nblk):
        xc_b, prev = _perm_conv(proj[b * PERM:(b + 1) * PERM, :D_LRU], prev, cw, cb)
        xcs.append(xc_b)
    halo_ref[...] = prev
    xc = jnp.concatenate(xcs, axis=0)
    xcb = xc.astype(BF16)
    neg_c_sp = (-LRU_C * LOG2E) * _softplus(-lam_ref[...])

    ys = []
    for n in range(LRU_BLOCKS):
        lanes = slice(n * LRU_BLOCK, (n + 1) * LRU_BLOCK)
        ai = jnp.dot(xcb[:, lanes], wai_ref[n], preferred_element_type=F32)
        produce_ref[:, n * piece:(n + 1) * piece] = jnp.dot(xp, w_ref[:, n * piece:(n + 1) * piece],
                                                            preferred_element_type=F32)
        r = _sigmoid(ai[:, :LRU_BLOCK] + ba_ref[:, lanes])
        i = _sigmoid(ai[:, LRU_BLOCK:] + bi_ref[:, lanes])
        a = jnp.exp2(neg_c_sp[:, lanes] * r)
        v = 1.0 - a * a
        u = jnp.where(v > 0.0, v * lax.rsqrt(v), 0.0) * (i * xc[:, lanes])
        carry = h_ref[:, lanes]
        hs = []
        for b in range(nblk):
            h_b, carry = _perm_scan(a[b * PERM:(b + 1) * PERM, :], u[b * PERM:(b + 1) * PERM, :], carry)
            hs.append(h_b)
        h_ref[:, lanes] = carry
        gate = proj[:, D_LRU + n * LRU_BLOCK:D_LRU + (n + 1) * LRU_BLOCK]
        ys.append((_gelu_tanh(gate) * jnp.concatenate(hs, axis=0)).astype(BF16))
    o_ref[...] = _permute_rows(permt_ref[...], jnp.concatenate(ys, axis=1))


def _lru_branch(x, layer, w_in, cw, cb, wai, ba, bi, lam):
    b, s, d = x.shape
    rows = min(LRU_TC, s)
    total = b * s // rows
    perms = _perm_matrices()
    params = (cw, cb, wai, ba, bi, lam)
    args = (x.reshape(total, rows, d),) + perms + (w_in,) + params
    y = pl.pallas_call(
        functools.partial(_lru_kernel, rows=rows, blocks_per_seq=s // rows),
        grid=(total + 1,),
        in_specs=[pl.BlockSpec((None, rows, d), lambda i: (jnp.minimum(i, total - 1), 0, 0))]
        + [_resident(a.shape) for a in perms] + [_layer(w_in, layer, (d, 2 * D_LRU))]
        + [_layer(a, layer) for a in params],
        out_specs=pl.BlockSpec((None, rows, D_LRU), lambda i: (jnp.maximum(i - 1, 0), 0, 0)),
        out_shape=jax.ShapeDtypeStruct((total, rows, D_LRU), BF16),
        scratch_shapes=[pltpu.VMEM((rows, 2 * D_LRU), F32), pltpu.VMEM((rows, 2 * D_LRU), F32),
                        pltpu.VMEM((HALO, D_LRU), F32), pltpu.VMEM((1, D_LRU), F32)],
        compiler_params=pltpu.CompilerParams(dimension_semantics=("arbitrary",), vmem_limit_bytes=VMEM_LIMIT),
        name="lru_branch",
    )(*args)
    return y.reshape(b, s, D_LRU)


def _head_expansion_matrix():
    e = np.zeros((2 * LANES, D_SSD), np.float32)
    for h in range(SSD_HEADS):
        for t in range(2):
            e[t * LANES + h, h * SSD_HEAD_DIM:(h + 1) * SSD_HEAD_DIM] = 1.0
    return jnp.asarray(e, BF16)


def _expand_heads(w, ehead):
    return jnp.dot(_split_bf16(w, 2), ehead, preferred_element_type=F32)


def _ssd_kernel(x_ref, perm_ref, permt_ref, w_ref, cw_ref, cb_ref, dtb_ref, alog_ref, dskip_ref, nw_ref, ehead_ref,
                o_ref, proj_a, proj_b, halo_ref, state_ref, *, rows, blocks_per_seq):
    @pl.when(_starts_sequence(blocks_per_seq))
    def _():
        halo_ref[...] = jnp.zeros_like(halo_ref)
        state_ref[...] = jnp.zeros_like(state_ref)

    _pipelined_steps(functools.partial(_ssd_step, x_ref, perm_ref, permt_ref, w_ref, cw_ref, cb_ref, dtb_ref, alog_ref,
                                       dskip_ref, nw_ref, ehead_ref, o_ref, halo_ref, state_ref, rows),
                     proj_a, proj_b)


def _ssd_step(x_ref, perm_ref, permt_ref, w_ref, cw_ref, cb_ref, dtb_ref, alog_ref, dskip_ref, nw_ref, ehead_ref, o_ref,
              halo_ref, state_ref, rows, produce_ref, proj):
    length = PERM
    xp = _permute_rows(perm_ref[...], x_ref[...].astype(BF16))
    n_pieces = (rows // length) * SSD_GROUPS
    mxu_cols = 2 * LANES
    n_tiles = pl.cdiv(w_ref.shape[1], mxu_cols)
    bounds = [min((i * n_tiles // n_pieces) * mxu_cols, w_ref.shape[1]) for i in range(n_pieces)] + [w_ref.shape[1]]
    dt = _softplus(proj[:, D_SSD + D_XBC:] + dtb_ref[...])
    da2 = dt * (-LOG2E * jnp.exp(alog_ref[...]))
    r_i = lax.broadcasted_iota(jnp.int32, (length, length), 0)
    c_i = lax.broadcasted_iota(jnp.int32, (length, length), 1)
    tok = lambda p: SEG * (p & (SUBLANES - 1)) + (p >> 3)
    causal = tok(r_i) >= tok(c_i)
    neg_mask = jnp.where(causal, 0.0, NON_CAUSAL_LOG2)
    lane = lax.broadcasted_iota(jnp.int32, (length, LANES), 1)
    first_head = lane < SSD_HEAD_DIM
    ehead = ehead_ref[...]
    cw = cw_ref[...]
    cb_conv = cb_ref[...]
    prev = halo_ref[...]

    for c in range(rows // length):
        rs = slice(c * length, (c + 1) * length)
        conv, prev = _perm_conv(proj[rs, D_SSD:D_SSD + D_XBC], prev, cw, cb_conv)
        xbc = _silu(conv)
        xs = xbc[:, :D_SSD]
        dt_c = dt[rs, :]
        b_t = [xbc[:, D_SSD + g * SSD_STATE:D_SSD + (g + 1) * SSD_STATE].T.astype(BF16) for g in range(SSD_GROUPS)]
        c_all = [xbc[:, D_SSD + D_BC + g * SSD_STATE:D_SSD + D_BC + (g + 1) * SSD_STATE].astype(BF16)
                 for g in range(SSD_GROUPS)]
        cb_all = [jnp.dot(c_all[g], b_t[g], preferred_element_type=F32) for g in range(SSD_GROUPS)]
        csl = _perm_cumsum(da2[rs, :])
        w_state = dt_c * jnp.exp2(csl[length - 1:length, :] - csl)
        w_out = jnp.exp2(csl)
        src_t = (csl - jnp.log2(dt_c)).T
        w_state_e = _expand_heads(w_state, ehead)
        w_out_e = _expand_heads(w_out, ehead)

        y_cols = []
        for g in range(SSD_GROUPS):
            lo, hi = bounds[c * SSD_GROUPS + g], bounds[c * SSD_GROUPS + g + 1]
            produce_ref[:, lo:hi] = jnp.dot(xp, w_ref[:, lo:hi], preferred_element_type=F32)
            tiles = []
            c_g = c_all[g]
            cb = cb_all[g]
            gl = slice(g * D_GROUP, (g + 1) * D_GROUP)
            heads_per_group = SSD_HEADS // SSD_GROUPS
            for p in range(heads_per_group // 2):
                h0 = g * heads_per_group + 2 * p
                ms = []
                for h in (h0, h0 + 1):
                    seg = jnp.broadcast_to(csl[:, h:h + 1], (length, length)) - src_t[h:h + 1, :] + neg_mask
                    ms.append((cb * jnp.exp2(seg)).astype(BF16))
                x_t = xs[:, h0 * SSD_HEAD_DIM:(h0 + 2) * SSD_HEAD_DIM]
                rhs = jnp.concatenate([jnp.where(first_head, x_t, 0.0), jnp.where(first_head, 0.0, x_t)],
                                      axis=0).astype(BF16)
                tiles.append(jnp.dot(jnp.concatenate(ms, axis=1), rhs, preferred_element_type=F32))
            st = state_ref[g]
            y_off = jnp.dot(c_g, st.astype(BF16), preferred_element_type=F32) * w_out_e[:, gl]
            xd = (xs[:, gl] * w_state_e[:, gl]).astype(BF16)
            s_new = jnp.dot(b_t[g], xd, preferred_element_type=F32)
            state_ref[g] = st * w_out_e[length - 1:length, gl] + s_new
            y_cols.append(jnp.concatenate(tiles, axis=1) + y_off)

        y = jnp.concatenate(y_cols, axis=1) + dskip_ref[...] * xs
        y = y * _silu(proj[rs, :D_SSD])
        outs = []
        for g in range(SSD_GROUPS):
            yg = y[:, g * D_GROUP:(g + 1) * D_GROUP]
            outs.append(yg * lax.rsqrt(jnp.mean(yg * yg, axis=-1, keepdims=True) + EPS))
        y_out = (jnp.concatenate(outs, axis=1) * nw_ref[...]).astype(BF16)
        o_ref[rs, :] = _permute_rows(permt_ref[...], y_out)
    halo_ref[...] = prev


def _ssd_branch(x, layer, w, cw, cb, dtb, alog, dskip, nw):
    b, s, d = x.shape
    rows = min(SSD_TC, s)
    total = b * s // rows
    width = w.shape[2]
    perms = _perm_matrices()
    params = (w, cw, cb, dtb, alog, dskip, nw)
    ehead = _head_expansion_matrix()
    args = (x.reshape(total, rows, d),) + perms + params + (ehead,)
    y = pl.pallas_call(
        functools.partial(_ssd_kernel, rows=rows, blocks_per_seq=s // rows),
        grid=(total + 1,),
        in_specs=[pl.BlockSpec((None, rows, d), lambda i: (jnp.minimum(i, total - 1), 0, 0))]
        + [_resident(a.shape) for a in perms] + [_layer(a, layer) for a in params] + [_resident(ehead.shape)],
        out_specs=pl.BlockSpec((None, rows, D_SSD), lambda i: (jnp.maximum(i - 1, 0), 0, 0)),
        out_shape=jax.ShapeDtypeStruct((total, rows, D_SSD), BF16),
        scratch_shapes=[pltpu.VMEM((rows, width), F32), pltpu.VMEM((rows, width), F32),
                        pltpu.VMEM((HALO, D_XBC), F32), pltpu.VMEM((SSD_GROUPS, SSD_STATE, D_GROUP), F32)],
        compiler_params=pltpu.CompilerParams(dimension_semantics=("arbitrary",), vmem_limit_bytes=VMEM_LIMIT),
        name="ssd_branch",
    )(*args)
    return y.reshape(b, s, D_SSD)


def _mix_kernel(x_ref, wqg_ref, kv_ref, ylru_ref, yssd_ref, bg_ref, wl_ref, ws_ref, wx_ref, wo_ref, g_ref, b_ref,
                o_ref, *, alpha, sub):
    dot = functools.partial(jnp.dot, preferred_element_type=F32)
    blocks = [slice(i * sub, (i + 1) * sub) for i in range(x_ref.shape[0] // sub)]
    xf = [x_ref[r, :] for r in blocks]
    pq = [dot(x.astype(BF16), wqg_ref[...]) for x in xf]
    scores = []
    for p in pq:
        for h in range(XA_HEADS):
            q_h = p[:, h * XA_HEAD_DIM:(h + 1) * XA_HEAD_DIM].astype(BF16)
            k_h = kv_ref[:, h * XA_HEAD_DIM:(h + 1) * XA_HEAD_DIM]
            scores.append(lax.dot_general(q_h, k_h, (((1,), (1,)), ((), ())), preferred_element_type=F32))
    br_lru = [dot(ylru_ref[r, :], wl_ref[...]) for r in blocks]
    br_ssd = [dot(yssd_ref[r, :], ws_ref[...]) for r in blocks]
    outs = []
    for sc in scores:
        e = jnp.exp2((sc - jnp.max(sc, axis=-1, keepdims=True)) * (LOG2E * XA_HEAD_DIM ** -0.5))
        probs = (e / jnp.sum(e, axis=-1, keepdims=True)).astype(BF16)
        h = len(outs) % XA_HEADS
        outs.append(dot(probs, kv_ref[:, D_XA + h * XA_HEAD_DIM:D_XA + (h + 1) * XA_HEAD_DIM]))
    br_xa = [dot(jnp.concatenate(outs[i * XA_HEADS:(i + 1) * XA_HEADS], axis=1).astype(BF16), wx_ref[...])
             for i in range(len(blocks))]
    merged = []
    for i, p in enumerate(pq):
        gates = _sigmoid(p[:, D_XA:] + bg_ref[...])
        merged.append((gates[:, :D_MODEL] * br_lru[i] + gates[:, D_MODEL:2 * D_MODEL] * br_ssd[i]
                       + gates[:, 2 * D_MODEL:] * br_xa[i]).astype(BF16))
    mix = [dot(m, wo_ref[...]) for m in merged]
    for i, r in enumerate(blocks):
        o_ref[r, :] = _layer_norm(alpha * xf[i] + mix[i], g_ref[...], b_ref[...])


def _mix_layer(x, layer, wqg, kv, ylru, yssd, bg, wl, ws, wx, wo, g, bb, alpha):
    b, s, d = x.shape
    rows = min(MIX_TM, s)
    m = kv.shape[1]
    tok = lambda width: pl.BlockSpec((None, rows, width), lambda i, j: (i, j, 0))
    return pl.pallas_call(
        functools.partial(_mix_kernel, alpha=alpha, sub=min(MIX_SUB, rows)),
        grid=(b, s // rows),
        in_specs=[tok(d), _layer(wqg, layer), pl.BlockSpec((None, m, kv.shape[2]), lambda i, j: (i, 0, 0)),
                  tok(D_LRU), tok(D_SSD)] + [_layer(a, layer) for a in (bg, wl, ws, wx, wo, g, bb)],
        out_specs=tok(d),
        out_shape=jax.ShapeDtypeStruct((b, s, d), F32),
        compiler_params=pltpu.CompilerParams(dimension_semantics=("arbitrary", "arbitrary"),
                                             vmem_limit_bytes=VMEM_LIMIT),
        name="mix_layer",
    )(x, wqg, kv, ylru, yssd, bg, wl, ws, wx, wo, g, bb)


def _ffn_kernel(x_ref, wi_ref, wd_ref, g_ref, b_ref, o_ref, *, alpha, sub):
    dot = functools.partial(jnp.dot, preferred_element_type=F32)
    blocks = [slice(i * sub, (i + 1) * sub) for i in range(x_ref.shape[0] // sub)]
    xf = [x_ref[r, :] for r in blocks]
    hid = [dot(x.astype(BF16), wi_ref[...]) for x in xf]
    down = [dot((_silu(h[:, :D_FF]) * h[:, D_FF:]).astype(BF16), wd_ref[...]) for h in hid]
    for i, r in enumerate(blocks):
        o_ref[r, :] = _layer_norm(alpha * xf[i] + down[i], g_ref[...], b_ref[...])


def _ffn_layer(x, layer, wi, wd, g, bb, alpha):
    b, s, d = x.shape
    rows = min(FFN_TM, s)
    tok = pl.BlockSpec((None, rows, d), lambda i, j: (i, j, 0))
    return pl.pallas_call(
        functools.partial(_ffn_kernel, alpha=alpha, sub=min(FFN_SUB, rows)),
        grid=(b, s // rows),
        in_specs=[tok] + [_layer(a, layer) for a in (wi, wd, g, bb)],
        out_specs=tok,
        out_shape=jax.ShapeDtypeStruct((b, s, d), F32),
        compiler_params=pltpu.CompilerParams(dimension_semantics=("arbitrary", "arbitrary"),
                                             vmem_limit_bytes=VMEM_LIMIT),
        name="ffn_layer",
    )(x, wi, wd, g, bb)


def kernel(x, mem, w_in, b_gate, lru_conv_w, lru_conv_b, lru_w_a, lru_b_a, lru_w_i, lru_b_i, lru_lambda, ssd_conv_w,
           ssd_conv_b, ssd_dt_bias, ssd_a_log, ssd_d, ssd_norm_w, mem_w_kv, w_br_lru, w_br_ssd, w_br_xa, w_out, ln1_g,
           ln1_b, ffn_w_in, ffn_w_down, ln2_g, ln2_b):
    depth = w_in.shape[0]
    alpha = (2 * depth) ** 0.25
    o_z = 2 * D_LRU
    o_xbc = o_z + D_SSD
    o_dt = o_xbc + D_XBC
    o_q = o_dt + SSD_HEADS

    w_in_bf = w_in.astype(BF16)
    w_ssd = w_in_bf[:, :, o_z:o_dt + DT_PAD]
    w_qg = w_in_bf[:, :, o_q:]
    w_ai = jnp.concatenate([lru_w_a, lru_w_i], axis=-1).astype(BF16)
    pad_heads = lambda a: jnp.pad(a, ((0, 0), (0, DT_PAD - SSD_HEADS)))[:, None, :]
    dtb = pad_heads(ssd_dt_bias)
    alog = pad_heads(ssd_a_log)
    dskip = jnp.repeat(ssd_d, SSD_HEAD_DIM, axis=-1)[:, None, :]
    row = lambda a: a[:, None, :]
    w_kv = mem_w_kv.astype(BF16)
    wl, ws, wx, wo = (w.astype(BF16) for w in (w_br_lru, w_br_ssd, w_br_xa, w_out))
    wi, wd = ffn_w_in.astype(BF16), ffn_w_down.astype(BF16)
    bg = b_gate.reshape(depth, 1, N_BRANCH * D_MODEL)
    lru_params = (lru_conv_w, row(lru_conv_b), w_ai, row(lru_b_a), row(lru_b_i), row(lru_lambda))
    ssd_params = (w_ssd, ssd_conv_w, row(ssd_conv_b), dtb, alog, dskip, row(ssd_norm_w))
    ln1 = (row(ln1_g), row(ln1_b))
    ln2 = (row(ln2_g), row(ln2_b))

    for l in range(depth):
        kv = _kv_proj(mem, w_kv, l)
        y_lru = _lru_branch(x, l, w_in_bf, *lru_params)
        y_ssd = _ssd_branch(x, l, *ssd_params)
        x = _mix_layer(x, l, w_qg, kv, y_lru, y_ssd, bg, wl, ws, wx, wo, *ln1, alpha)
        x = _ffn_layer(x, l, wi, wd, *ln2, alpha)
    return x
```

```python
import functools

import numpy as np
import jax
import jax.numpy as jnp
from jax import lax
from jax.experimental import pallas as pl
from jax.experimental.pallas import tpu as pltpu

F32 = jnp.float32
BF16 = jnp.bfloat16

D_MODEL = 1024
CONV_WIDTH = 4
D_LRU = D_MODEL
LRU_BLOCKS = 8
LRU_BLOCK = D_LRU // LRU_BLOCKS
LRU_C = 8.0
D_SSD = 2 * D_MODEL
SSD_HEAD_DIM = 64
SSD_HEADS = D_SSD // SSD_HEAD_DIM
SSD_GROUPS = 4
SSD_STATE = 128
D_BC = SSD_GROUPS * SSD_STATE
D_XBC = D_SSD + 2 * D_BC
D_GROUP = D_SSD // SSD_GROUPS
XA_HEADS = 4
XA_HEAD_DIM = 256
D_XA = XA_HEADS * XA_HEAD_DIM
N_BRANCH = 3
D_FF = ((8 * D_MODEL // 3 + 255) // 256) * 256
EPS = 1e-5

SUBLANES = 8
LANES = 128
PERM = 128
SEG = PERM // SUBLANES
HALO = (CONV_WIDTH - 1) * SUBLANES
DT_PAD = LANES

LRU_TC = 512
SSD_TC = 256
MIX_TM = 512
MIX_SUB = 256
FFN_TM = 1024
FFN_SUB = 256
VMEM_LIMIT = 56 * 1024 * 1024


def _resident(shape):
    nd = len(shape)
    return pl.BlockSpec(shape, lambda *_: (0,) * nd, pipeline_mode=pl.Buffered(1))


def _layer(arr, layer, block=None, col=0):
    shape = tuple(arr.shape[1:] if block is None else block)
    index = (layer,) + (0,) * (len(shape) - 1) + (col,)
    return pl.BlockSpec((None,) + shape, lambda *_: index, pipeline_mode=pl.Buffered(1))


def _cast_once(dst_ref, srcs):
    @pl.when(pl.program_id(0) == 0)
    def _():
        at = 0
        for src in srcs:
            width = src.shape[1]
            dst_ref[:, at:at + width] = src[...].astype(BF16)
            at += width


LOG2E = 1.4426950408889634
NON_CAUSAL_LOG2 = -1e30


def _sigmoid(x):
    return 1.0 / (1.0 + jnp.exp2(x * -LOG2E))


def _silu(x):
    return x * _sigmoid(x)


def _softplus(x):
    return jnp.maximum(x, 0.0) + jnp.log1p(jnp.exp(-jnp.abs(x)))


def _gelu_tanh(x):
    return 0.5 * x * (1.0 + jnp.tanh(0.7978845608028654 * (x + 0.044715 * (x * x * x))))


def _layer_norm(v, g, b):
    mu = jnp.mean(v, axis=-1, keepdims=True)
    c = v - mu
    var = jnp.mean(c * c, axis=-1, keepdims=True)
    return c * lax.rsqrt(var + EPS) * g + b


def _perm_matrices():
    p = np.zeros((PERM, PERM), np.float32)
    for pos in range(PERM):
        j, m = divmod(pos, SUBLANES)
        p[pos, SEG * m + j] = 1.0
    return jnp.asarray(p, BF16), jnp.asarray(p.T, BF16)


def _permute_rows(perm, v):
    blocks = [jnp.dot(perm, v[b * PERM:(b + 1) * PERM, :], preferred_element_type=F32).astype(BF16)
              for b in range(v.shape[0] // PERM)]
    return blocks[0] if len(blocks) == 1 else jnp.concatenate(blocks, axis=0)


def _perm_conv(cur, prev_rolled, cw, cb):
    c = cur.shape[1]
    taps = CONV_WIDTH - 1
    rolled = pltpu.roll(cur[PERM - HALO:, :].reshape(taps, SUBLANES, c), 1, 1)
    sub = lax.broadcasted_iota(jnp.int32, (taps, SUBLANES, c), 1)
    halo = jnp.where(sub == 0, prev_rolled.reshape(taps, SUBLANES, c), rolled).reshape(HALO, c)
    ext = jnp.concatenate([halo, cur], axis=0)
    y = cb + cw[taps:taps + 1, :] * cur
    for k in range(1, CONV_WIDTH):
        y = y + cw[taps - k:taps - k + 1, :] * ext[HALO - SUBLANES * k:HALO - SUBLANES * k + PERM, :]
    return y, rolled.reshape(HALO, c)


def _split_bf16(v, terms):
    parts = []
    rem = v
    for _ in range(terms):
        p = rem.astype(BF16)
        parts.append(p)
        rem = rem - p.astype(F32)
    return jnp.concatenate(parts, axis=1)


def _kv_kernel(mem_ref, w_ref, o_ref):
    o_ref[...] = jnp.dot(mem_ref[...].astype(BF16), w_ref[...], preferred_element_type=F32).astype(BF16)


def _kv_proj(mem, w_kv, layer):
    b, m, d = mem.shape
    n = w_kv.shape[2]
    return pl.pallas_call(
        _kv_kernel,
        grid=(b,),
        in_specs=[pl.BlockSpec((None, m, d), lambda i: (i, 0, 0)), _layer(w_kv, layer)],
        out_specs=pl.BlockSpec((None, m, n), lambda i: (i, 0, 0)),
        out_shape=jax.ShapeDtypeStruct((b, m, n), BF16),
        compiler_params=pltpu.CompilerParams(dimension_semantics=("arbitrary",), vmem_limit_bytes=VMEM_LIMIT),
        name="kv_proj",
    )(mem, w_kv)


def _perm_cumsum(v):
    width = v.shape[1]
    v3 = v.reshape(SEG, SUBLANES, width)
    loc = [v3[0]]
    for j in range(1, SEG):
        loc.append(loc[-1] + v3[j])
    ends = loc[-1]
    carry = jnp.zeros((1, width), F32)
    starts = []
    for m in range(SUBLANES):
        starts.append(carry)
        carry = carry + ends[m:m + 1, :]
    start = jnp.concatenate(starts, axis=0)
    return jnp.stack([s + start for s in loc], axis=0).reshape(PERM, width)


def _perm_scan(a, u, h_in):
    width = a.shape[1]
    a3 = a.reshape(SEG, SUBLANES, width)
    u3 = u.reshape(SEG, SUBLANES, width)
    h_loc = [u3[0]]
    p_loc = [a3[0]]
    for j in range(1, SEG):
        h_loc.append(a3[j] * h_loc[-1] + u3[j])
        p_loc.append(a3[j] * p_loc[-1])
    ends_h, ends_p = h_loc[-1], p_loc[-1]
    carry = h_in
    starts = []
    for m in range(SUBLANES):
        starts.append(carry)
        carry = ends_h[m:m + 1, :] + ends_p[m:m + 1, :] * carry
    start = jnp.concatenate(starts, axis=0)
    h = jnp.stack([h_loc[j] + p_loc[j] * start for j in range(SEG)], axis=0)
    return h.reshape(PERM, width), carry


def _pipelined_steps(step, proj_a, proj_b):
    s = pl.program_id(0)

    @pl.when(s == 0)
    def _():
        proj_b[...] = jnp.zeros_like(proj_b)

    @pl.when(s % 2 == 0)
    def _():
        step(proj_a, proj_b)

    @pl.when(s % 2 == 1)
    def _():
        step(proj_b, proj_a)


def _starts_sequence(blocks_per_seq):
    s = pl.program_id(0)
    return jnp.logical_or(s == 0, (s + blocks_per_seq - 1) % blocks_per_seq == 0)


def _lru_kernel(x_ref, perm_ref, permt_ref, w_ref, cw_ref, cb_ref, wai_ref, ba_ref, bi_ref, lam_ref, o_ref,
                proj_a, proj_b, halo_ref, h_ref, wbf_ref, *, rows, blocks_per_seq):
    @pl.when(_starts_sequence(blocks_per_seq))
    def _():
        halo_ref[...] = jnp.zeros_like(halo_ref)
        h_ref[...] = jnp.zeros_like(h_ref)

    _cast_once(wbf_ref, [w_ref])
    _pipelined_steps(functools.partial(_lru_step, x_ref, perm_ref, permt_ref, wbf_ref, cw_ref, cb_ref, wai_ref, ba_ref,
                                       bi_ref, lam_ref, o_ref, halo_ref, h_ref, rows), proj_a, proj_b)


def _lru_step(x_ref, perm_ref, permt_ref, w_ref, cw_ref, cb_ref, wai_ref, ba_ref, bi_ref, lam_ref, o_ref, halo_ref,
              h_ref, rows, produce_ref, proj):
    nblk = rows // PERM
    xp = _permute_rows(perm_ref[...], x_ref[...].astype(BF16))
    piece = 2 * D_LRU // LRU_BLOCKS
    cw = cw_ref[...]
    cb = cb_ref[...]
    prev = halo_ref[...]
    xcs = []
    for b in range(nblk):
        xc_b, prev = _perm_conv(proj[b * PERM:(b + 1) * PERM, :D_LRU], prev, cw, cb)
        xcs.append(xc_b)
    halo_ref[...] = prev
    xc = jnp.concatenate(xcs, axis=0)
    xcb = xc.astype(BF16)
    neg_c_sp = (-LRU_C * LOG2E) * _softplus(-lam_ref[...])

    ys = []
    for n in range(LRU_BLOCKS):
        lanes = slice(n * LRU_BLOCK, (n + 1) * LRU_BLOCK)
        ai = jnp.dot(xcb[:, lanes], wai_ref[n], preferred_element_type=F32)
        produce_ref[:, n * piece:(n + 1) * piece] = jnp.dot(xp, w_ref[:, n * piece:(n + 1) * piece],
                                                            preferred_element_type=F32)
        r = _sigmoid(ai[:, :LRU_BLOCK] + ba_ref[:, lanes])
        i = _sigmoid(ai[:, LRU_BLOCK:] + bi_ref[:, lanes])
        a = jnp.exp2(neg_c_sp[:, lanes] * r)
        v = 1.0 - a * a
        u = jnp.where(v > 0.0, v * lax.rsqrt(v), 0.0) * (i * xc[:, lanes])
        carry = h_ref[:, lanes]
        hs = []
        for b in range(nblk):
            h_b, carry = _perm_scan(a[b * PERM:(b + 1) * PERM, :], u[b * PERM:(b + 1) * PERM, :], carry)
            hs.append(h_b)
        h_ref[:, lanes] = carry
        gate = proj[:, D_LRU + n * LRU_BLOCK:D_LRU + (n + 1) * LRU_BLOCK]
        ys.append((_gelu_tanh(gate) * jnp.concatenate(hs, axis=0)).astype(BF16))
    o_ref[...] = _permute_rows(permt_ref[...], jnp.concatenate(ys, axis=1))


def _lru_branch(x, layer, w_in, cw, cb, wai, ba, bi, lam):
    b, s, d = x.shape
    rows = min(LRU_TC, s)
    total = b * s // rows
    perms = _perm_matrices()
    params = (cw, cb, wai, ba, bi, lam)
    args = (x.reshape(total, rows, d),) + perms + (w_in,) + params
    y = pl.pallas_call(
        functools.partial(_lru_kernel, rows=rows, blocks_per_seq=s // rows),
        grid=(total + 1,),
        in_specs=[pl.BlockSpec((None, rows, d), lambda i: (jnp.minimum(i, total - 1), 0, 0))]
        + [_resident(a.shape) for a in perms] + [_layer(w_in, layer, (d, 2 * D_LRU))]
        + [_layer(a, layer) for a in params],
        out_specs=pl.BlockSpec((None, rows, D_LRU), lambda i: (jnp.maximum(i - 1, 0), 0, 0)),
        out_shape=jax.ShapeDtypeStruct((total, rows, D_LRU), BF16),
        scratch_shapes=[pltpu.VMEM((rows, 2 * D_LRU), F32), pltpu.VMEM((rows, 2 * D_LRU), F32),
                        pltpu.VMEM((HALO, D_LRU), F32), pltpu.VMEM((1, D_LRU), F32),
                        pltpu.VMEM((d, 2 * D_LRU), BF16)],
        compiler_params=pltpu.CompilerParams(dimension_semantics=("arbitrary",), vmem_limit_bytes=VMEM_LIMIT),
        name="lru_branch",
    )(*args)
    return y.reshape(b, s, D_LRU)


def _head_expansion_matrix():
    e = np.zeros((2 * LANES, D_SSD), np.float32)
    for h in range(SSD_HEADS):
        for t in range(2):
            e[t * LANES + h, h * SSD_HEAD_DIM:(h + 1) * SSD_HEAD_DIM] = 1.0
    return jnp.asarray(e, BF16)


def _expand_heads(w, ehead):
    return jnp.dot(_split_bf16(w, 2), ehead, preferred_element_type=F32)


def _ssd_kernel(x_ref, perm_ref, permt_ref, *rest, rows, blocks_per_seq, n_w):
    w_blocks = rest[:n_w]
    (cw_ref, cb_ref, dtb_ref, alog_ref, dskip_ref, nw_ref, ehead_ref, o_ref, proj_a, proj_b, halo_ref, state_ref,
     wbf_ref) = rest[n_w:]

    @pl.when(_starts_sequence(blocks_per_seq))
    def _():
        halo_ref[...] = jnp.zeros_like(halo_ref)
        state_ref[...] = jnp.zeros_like(state_ref)

    _cast_once(wbf_ref, w_blocks)
    _pipelined_steps(functools.partial(_ssd_step, x_ref, perm_ref, permt_ref, wbf_ref, cw_ref, cb_ref, dtb_ref, alog_ref,
                                       dskip_ref, nw_ref, ehead_ref, o_ref, halo_ref, state_ref, rows),
                     proj_a, proj_b)


def _ssd_step(x_ref, perm_ref, permt_ref, w_ref, cw_ref, cb_ref, dtb_ref, alog_ref, dskip_ref, nw_ref, ehead_ref, o_ref,
              halo_ref, state_ref, rows, produce_ref, proj):
    length = PERM
    xp = _permute_rows(perm_ref[...], x_ref[...].astype(BF16))
    n_pieces = (rows // length) * SSD_GROUPS
    mxu_cols = 2 * LANES
    n_tiles = pl.cdiv(w_ref.shape[1], mxu_cols)
    bounds = [min((i * n_tiles // n_pieces) * mxu_cols, w_ref.shape[1]) for i in range(n_pieces)] + [w_ref.shape[1]]
    dt = _softplus(proj[:, D_SSD + D_XBC:] + dtb_ref[...])
    da2 = dt * (-LOG2E * jnp.exp(alog_ref[...]))
    r_i = lax.broadcasted_iota(jnp.int32, (length, length), 0)
    c_i = lax.broadcasted_iota(jnp.int32, (length, length), 1)
    tok = lambda p: SEG * (p & (SUBLANES - 1)) + (p >> 3)
    causal = tok(r_i) >= tok(c_i)
    neg_mask = jnp.where(causal, 0.0, NON_CAUSAL_LOG2)
    lane = lax.broadcasted_iota(jnp.int32, (length, LANES), 1)
    first_head = lane < SSD_HEAD_DIM
    ehead = ehead_ref[...]
    cw = cw_ref[...]
    cb_conv = cb_ref[...]
    prev = halo_ref[...]

    for c in range(rows // length):
        rs = slice(c * length, (c + 1) * length)
        conv, prev = _perm_conv(proj[rs, D_SSD:D_SSD + D_XBC], prev, cw, cb_conv)
        xbc = _silu(conv)
        xs = xbc[:, :D_SSD]
        dt_c = dt[rs, :]
        b_t = [xbc[:, D_SSD + g * SSD_STATE:D_SSD + (g + 1) * SSD_STATE].T.astype(BF16) for g in range(SSD_GROUPS)]
        c_all = [xbc[:, D_SSD + D_BC + g * SSD_STATE:D_SSD + D_BC + (g + 1) * SSD_STATE].astype(BF16)
                 for g in range(SSD_GROUPS)]
        cb_all = [jnp.dot(c_all[g], b_t[g], preferred_element_type=F32) for g in range(SSD_GROUPS)]
        csl = _perm_cumsum(da2[rs, :])
        w_state = dt_c * jnp.exp2(csl[length - 1:length, :] - csl)
        w_out = jnp.exp2(csl)
        src_t = (csl - jnp.log2(dt_c)).T
        w_state_e = _expand_heads(w_state, ehead)
        w_out_e = _expand_heads(w_out, ehead)

        y_cols = []
        for g in range(SSD_GROUPS):
            lo, hi = bounds[c * SSD_GROUPS + g], bounds[c * SSD_GROUPS + g + 1]
            produce_ref[:, lo:hi] = jnp.dot(xp, w_ref[:, lo:hi], preferred_element_type=F32)
            tiles = []
            c_g = c_all[g]
            cb = cb_all[g]
            gl = slice(g * D_GROUP, (g + 1) * D_GROUP)
            heads_per_group = SSD_HEADS // SSD_GROUPS
            for p in range(heads_per_group // 2):
                h0 = g * heads_per_group + 2 * p
                ms = []
                for h in (h0, h0 + 1):
                    seg = jnp.broadcast_to(csl[:, h:h + 1], (length, length)) - src_t[h:h + 1, :] + neg_mask
                    ms.append((cb * jnp.exp2(seg)).astype(BF16))
                x_t = xs[:, h0 * SSD_HEAD_DIM:(h0 + 2) * SSD_HEAD_DIM]
                rhs = jnp.concatenate([jnp.where(first_head, x_t, 0.0), jnp.where(first_head, 0.0, x_t)],
                                      axis=0).astype(BF16)
                tiles.append(jnp.dot(jnp.concatenate(ms, axis=1), rhs, preferred_element_type=F32))
            st = state_ref[g]
            y_off = jnp.dot(c_g, st.astype(BF16), preferred_element_type=F32) * w_out_e[:, gl]
            xd = (xs[:, gl] * w_state_e[:, gl]).astype(BF16)
            s_new = jnp.dot(b_t[g], xd, preferred_element_type=F32)
            state_ref[g] = st * w_out_e[length - 1:length, gl] + s_new
            y_cols.append(jnp.concatenate(tiles, axis=1) + y_off)

        y = jnp.concatenate(y_cols, axis=1) + dskip_ref[...] * xs
        y = y * _silu(proj[rs, :D_SSD])
        outs = []
        for g in range(SSD_GROUPS):
            yg = y[:, g * D_GROUP:(g + 1) * D_GROUP]
            outs.append(yg * lax.rsqrt(jnp.mean(yg * yg, axis=-1, keepdims=True) + EPS))
        y_out = (jnp.concatenate(outs, axis=1) * nw_ref[...]).astype(BF16)
        o_ref[rs, :] = _permute_rows(permt_ref[...], y_out)
    halo_ref[...] = prev


def _ssd_branch(x, layer, w_in, cw, cb, dtb, alog, dskip, nw):
    b, s, d = x.shape
    rows = min(SSD_TC, s)
    total = b * s // rows
    tail = (D_BC * 2 + DT_PAD) // 3
    o_z = 2 * D_LRU
    w_specs = [_layer(w_in, layer, (d, D_SSD), o_z // D_SSD), _layer(w_in, layer, (d, D_SSD), o_z // D_SSD + 1)]
    w_specs += [_layer(w_in, layer, (d, tail), (o_z + 2 * D_SSD) // tail + k) for k in range(3)]
    assert o_z % D_SSD == 0 and (o_z + 2 * D_SSD) % tail == 0 and tail % LANES == 0
    width = 2 * D_SSD + 3 * tail
    perms = _perm_matrices()
    params = (cw, cb, dtb, alog, dskip, nw)
    ehead = _head_expansion_matrix()
    args = (x.reshape(total, rows, d),) + perms + (w_in,) * len(w_specs) + params + (ehead,)
    y = pl.pallas_call(
        functools.partial(_ssd_kernel, rows=rows, blocks_per_seq=s // rows, n_w=len(w_specs)),
        grid=(total + 1,),
        in_specs=[pl.BlockSpec((None, rows, d), lambda i: (jnp.minimum(i, total - 1), 0, 0))]
        + [_resident(a.shape) for a in perms] + w_specs + [_layer(a, layer) for a in params]
        + [_resident(ehead.shape)],
        out_specs=pl.BlockSpec((None, rows, D_SSD), lambda i: (jnp.maximum(i - 1, 0), 0, 0)),
        out_shape=jax.ShapeDtypeStruct((total, rows, D_SSD), BF16),
        scratch_shapes=[pltpu.VMEM((rows, width), F32), pltpu.VMEM((rows, width), F32),
                        pltpu.VMEM((HALO, D_XBC), F32), pltpu.VMEM((SSD_GROUPS, SSD_STATE, D_GROUP), F32),
                        pltpu.VMEM((d, width), BF16)],
        compiler_params=pltpu.CompilerParams(dimension_semantics=("arbitrary",), vmem_limit_bytes=VMEM_LIMIT),
        name="ssd_branch",
    )(*args)
    return y.reshape(b, s, D_SSD)


def _mix_kernel(x_ref, wqg_ref, kv_ref, ylru_ref, yssd_ref, bg_ref, wl_ref, ws_ref, wx_ref, wo_ref, g_ref, b_ref,
                o_ref, *, alpha, sub):
    dot = functools.partial(jnp.dot, preferred_element_type=F32)
    blocks = [slice(i * sub, (i + 1) * sub) for i in range(x_ref.shape[0] // sub)]
    xf = [x_ref[r, :] for r in blocks]
    pq = [dot(x.astype(BF16), wqg_ref[...]) for x in xf]
    scores = []
    for p in pq:
        for h in range(XA_HEADS):
            q_h = p[:, h * XA_HEAD_DIM:(h + 1) * XA_HEAD_DIM].astype(BF16)
            k_h = kv_ref[:, h * XA_HEAD_DIM:(h + 1) * XA_HEAD_DIM]
            scores.append(lax.dot_general(q_h, k_h, (((1,), (1,)), ((), ())), preferred_element_type=F32))
    br_lru = [dot(ylru_ref[r, :], wl_ref[...]) for r in blocks]
    br_ssd = [dot(yssd_ref[r, :], ws_ref[...]) for r in blocks]
    outs = []
    for sc in scores:
        e = jnp.exp2((sc - jnp.max(sc, axis=-1, keepdims=True)) * (LOG2E * XA_HEAD_DIM ** -0.5))
        probs = (e / jnp.sum(e, axis=-1, keepdims=True)).astype(BF16)
        h = len(outs) % XA_HEADS
        outs.append(dot(probs, kv_ref[:, D_XA + h * XA_HEAD_DIM:D_XA + (h + 1) * XA_HEAD_DIM]))
    br_xa = [dot(jnp.concatenate(outs[i * XA_HEADS:(i + 1) * XA_HEADS], axis=1).astype(BF16), wx_ref[...])
             for i in range(len(blocks))]
    merged = []
    for i, p in enumerate(pq):
        gates = _sigmoid(p[:, D_XA:] + bg_ref[...])
        merged.append((gates[:, :D_MODEL] * br_lru[i] + gates[:, D_MODEL:2 * D_MODEL] * br_ssd[i]
                       + gates[:, 2 * D_MODEL:] * br_xa[i]).astype(BF16))
    mix = [dot(m, wo_ref[...]) for m in merged]
    for i, r in enumerate(blocks):
        o_ref[r, :] = _layer_norm(alpha * xf[i] + mix[i], g_ref[...], b_ref[...])


def _mix_layer(x, layer, wqg, kv, ylru, yssd, bg, wl, ws, wx, wo, g, bb, alpha):
    b, s, d = x.shape
    rows = min(MIX_TM, s)
    m = kv.shape[1]
    tok = lambda width: pl.BlockSpec((None, rows, width), lambda i, j: (i, j, 0))
    return pl.pallas_call(
        functools.partial(_mix_kernel, alpha=alpha, sub=min(MIX_SUB, rows)),
        grid=(b, s // rows),
        in_specs=[tok(d), _layer(wqg, layer), pl.BlockSpec((None, m, kv.shape[2]), lambda i, j: (i, 0, 0)),
                  tok(D_LRU), tok(D_SSD)] + [_layer(a, layer) for a in (bg, wl, ws, wx, wo, g, bb)],
        out_specs=tok(d),
        out_shape=jax.ShapeDtypeStruct((b, s, d), F32),
        compiler_params=pltpu.CompilerParams(dimension_semantics=("arbitrary", "arbitrary"),
                                             vmem_limit_bytes=VMEM_LIMIT),
        name="mix_layer",
    )(x, wqg, kv, ylru, yssd, bg, wl, ws, wx, wo, g, bb)


def _ffn_kernel(x_ref, wi_ref, wd_ref, g_ref, b_ref, o_ref, *, alpha, sub):
    dot = functools.partial(jnp.dot, preferred_element_type=F32)
    blocks = [slice(i * sub, (i + 1) * sub) for i in range(x_ref.shape[0] // sub)]
    xf = [x_ref[r, :] for r in blocks]
    hid = [dot(x.astype(BF16), wi_ref[...]) for x in xf]
    down = [dot((_silu(h[:, :D_FF]) * h[:, D_FF:]).astype(BF16), wd_ref[...]) for h in hid]
    for i, r in enumerate(blocks):
        o_ref[r, :] = _layer_norm(alpha * xf[i] + down[i], g_ref[...], b_ref[...])


def _ffn_layer(x, layer, wi, wd, g, bb, alpha):
    b, s, d = x.shape
    rows = min(FFN_TM, s)
    tok = pl.BlockSpec((None, rows, d), lambda i, j: (i, j, 0))
    return pl.pallas_call(
        functools.partial(_ffn_kernel, alpha=alpha, sub=min(FFN_SUB, rows)),
        grid=(b, s // rows),
        in_specs=[tok] + [_layer(a, layer) for a in (wi, wd, g, bb)],
        out_specs=tok,
        out_shape=jax.ShapeDtypeStruct((b, s, d), F32),
        compiler_params=pltpu.CompilerParams(dimension_semantics=("arbitrary", "arbitrary"),
                                             vmem_limit_bytes=VMEM_LIMIT),
        name="ffn_layer",
    )(x, wi, wd, g, bb)


def kernel(x, mem, w_in, b_gate, lru_conv_w, lru_conv_b, lru_w_a, lru_b_a, lru_w_i, lru_b_i, lru_lambda, ssd_conv_w,
           ssd_conv_b, ssd_dt_bias, ssd_a_log, ssd_d, ssd_norm_w, mem_w_kv, w_br_lru, w_br_ssd, w_br_xa, w_out, ln1_g,
           ln1_b, ffn_w_in, ffn_w_down, ln2_g, ln2_b):
    depth = w_in.shape[0]
    alpha = (2 * depth) ** 0.25
    o_z = 2 * D_LRU
    o_xbc = o_z + D_SSD
    o_dt = o_xbc + D_XBC
    o_q = o_dt + SSD_HEADS

    w_qg = w_in[:, :, o_q:].astype(BF16)
    w_ai = jnp.concatenate([lru_w_a, lru_w_i], axis=-1).astype(BF16)
    pad_heads = lambda a: jnp.pad(a, ((0, 0), (0, DT_PAD - SSD_HEADS)))[:, None, :]
    dtb = pad_heads(ssd_dt_bias)
    alog = pad_heads(ssd_a_log)
    dskip = jnp.repeat(ssd_d, SSD_HEAD_DIM, axis=-1)[:, None, :]
    row = lambda a: a[:, None, :]
    w_kv = mem_w_kv.astype(BF16)
    wl, ws, wx, wo = (w.astype(BF16) for w in (w_br_lru, w_br_ssd, w_br_xa, w_out))
    wi, wd = ffn_w_in.astype(BF16), ffn_w_down.astype(BF16)
    bg = b_gate.reshape(depth, 1, N_BRANCH * D_MODEL)
    lru_params = (lru_conv_w, row(lru_conv_b), w_ai, row(lru_b_a), row(lru_b_i), row(lru_lambda))
    ssd_params = (ssd_conv_w, row(ssd_conv_b), dtb, alog, dskip, row(ssd_norm_w))
    ln1 = (row(ln1_g), row(ln1_b))
    ln2 = (row(ln2_g), row(ln2_b))

    for l in range(depth):
        kv = _kv_proj(mem, w_kv, l)
        y_lru = _lru_branch(x, l, w_in, *lru_params)
        y_ssd = _ssd_branch(x, l, w_in, *ssd_params)
        x = _mix_layer(x, l, w_qg, kv, y_lru, y_ssd, bg, wl, ws, wx, wo, *ln1, alpha)
        x = _ffn_layer(x, l, wi, wd, *ln2, alpha)
    return x
```

```python
import functools

import numpy as np
import jax
import jax.numpy as jnp
from jax import lax
from jax.experimental import pallas as pl
from jax.experimental.pallas import tpu as pltpu

F32 = jnp.float32
BF16 = jnp.bfloat16

D_MODEL = 1024
CONV_WIDTH = 4
D_LRU = D_MODEL
LRU_BLOCKS = 8
LRU_BLOCK = D_LRU // LRU_BLOCKS
LRU_C = 8.0
D_SSD = 2 * D_MODEL
SSD_HEAD_DIM = 64
SSD_HEADS = D_SSD // SSD_HEAD_DIM
SSD_GROUPS = 4
SSD_STATE = 128
D_BC = SSD_GROUPS * SSD_STATE
D_XBC = D_SSD + 2 * D_BC
D_GROUP = D_SSD // SSD_GROUPS
XA_HEADS = 4
XA_HEAD_DIM = 256
D_XA = XA_HEADS * XA_HEAD_DIM
N_BRANCH = 3
D_FF = ((8 * D_MODEL // 3 + 255) // 256) * 256
EPS = 1e-5

SUBLANES = 8
LANES = 128
PERM = 128
SEG = PERM // SUBLANES
HALO = (CONV_WIDTH - 1) * SUBLANES
DT_PAD = LANES

LRU_TC = 512
SSD_TC = 256
MIX_TM = 512
MIX_SUB = 256
FFN_TM = 1024
FFN_SUB = 256
VMEM_LIMIT = 56 * 1024 * 1024


def _resident(shape):
    nd = len(shape)
    return pl.BlockSpec(shape, lambda *_: (0,) * nd, pipeline_mode=pl.Buffered(1))


def _layer(arr, layer, block=None):
    shape = tuple(arr.shape[1:] if block is None else block)
    return pl.BlockSpec((None,) + shape, lambda *_: (layer,) + (0,) * len(shape), pipeline_mode=pl.Buffered(1))


LOG2E = 1.4426950408889634
NON_CAUSAL_LOG2 = -1e30


def _sigmoid(x):
    return 1.0 / (1.0 + jnp.exp2(x * -LOG2E))


def _silu(x):
    return x * _sigmoid(x)


def _softplus(x):
    return jnp.maximum(x, 0.0) + jnp.log1p(jnp.exp(-jnp.abs(x)))


def _gelu_tanh(x):
    return 0.5 * x * (1.0 + jnp.tanh(0.7978845608028654 * (x + 0.044715 * (x * x * x))))


def _layer_norm(v, g, b):
    mu = jnp.mean(v, axis=-1, keepdims=True)
    c = v - mu
    var = jnp.mean(c * c, axis=-1, keepdims=True)
    return c * lax.rsqrt(var + EPS) * g + b


def _perm_matrices():
    p = np.zeros((PERM, PERM), np.float32)
    for pos in range(PERM):
        j, m = divmod(pos, SUBLANES)
        p[pos, SEG * m + j] = 1.0
    return jnp.asarray(p, BF16), jnp.asarray(p.T, BF16)


def _permute_rows(perm, v):
    blocks = [jnp.dot(perm, v[b * PERM:(b + 1) * PERM, :], preferred_element_type=F32).astype(BF16)
              for b in range(v.shape[0] // PERM)]
    return blocks[0] if len(blocks) == 1 else jnp.concatenate(blocks, axis=0)


def _perm_conv(cur, prev_rolled, cw, cb):
    c = cur.shape[1]
    taps = CONV_WIDTH - 1
    rolled = pltpu.roll(cur[PERM - HALO:, :].reshape(taps, SUBLANES, c), 1, 1)
    sub = lax.broadcasted_iota(jnp.int32, (taps, SUBLANES, c), 1)
    halo = jnp.where(sub == 0, prev_rolled.reshape(taps, SUBLANES, c), rolled).reshape(HALO, c)
    ext = jnp.concatenate([halo, cur], axis=0)
    y = cb + cw[taps:taps + 1, :] * cur
    for k in range(1, CONV_WIDTH):
        y = y + cw[taps - k:taps - k + 1, :] * ext[HALO - SUBLANES * k:HALO - SUBLANES * k + PERM, :]
    return y, rolled.reshape(HALO, c)


def _split_bf16(v, terms):
    parts = []
    rem = v
    for _ in range(terms):
        p = rem.astype(BF16)
        parts.append(p)
        rem = rem - p.astype(F32)
    return jnp.concatenate(parts, axis=1)


def _kv_kernel(mem_ref, w_ref, o_ref):
    o_ref[...] = jnp.dot(mem_ref[...].astype(BF16), w_ref[...], preferred_element_type=F32).astype(BF16)


def _kv_proj(mem, w_kv, layer):
    b, m, d = mem.shape
    n = w_kv.shape[2]
    return pl.pallas_call(
        _kv_kernel,
        grid=(b,),
        in_specs=[pl.BlockSpec((None, m, d), lambda i: (i, 0, 0)), _layer(w_kv, layer)],
        out_specs=pl.BlockSpec((None, m, n), lambda i: (i, 0, 0)),
        out_shape=jax.ShapeDtypeStruct((b, m, n), BF16),
        compiler_params=pltpu.CompilerParams(dimension_semantics=("arbitrary",), vmem_limit_bytes=VMEM_LIMIT),
        name="kv_proj",
    )(mem, w_kv)


def _perm_cumsum(v):
    width = v.shape[1]
    v3 = v.reshape(SEG, SUBLANES, width)
    loc = [v3[0]]
    for j in range(1, SEG):
        loc.append(loc[-1] + v3[j])
    ends = loc[-1]
    carry = jnp.zeros((1, width), F32)
    starts = []
    for m in range(SUBLANES):
        starts.append(carry)
        carry = carry + ends[m:m + 1, :]
    start = jnp.concatenate(starts, axis=0)
    return jnp.stack([s + start for s in loc], axis=0).reshape(PERM, width)


def _perm_scan(a, u, h_in):
    width = a.shape[1]
    a3 = a.reshape(SEG, SUBLANES, width)
    u3 = u.reshape(SEG, SUBLANES, width)
    h_loc = [u3[0]]
    p_loc = [a3[0]]
    for j in range(1, SEG):
        h_loc.append(a3[j] * h_loc[-1] + u3[j])
        p_loc.append(a3[j] * p_loc[-1])
    ends_h, ends_p = h_loc[-1], p_loc[-1]
    carry = h_in
    starts = []
    for m in range(SUBLANES):
        starts.append(carry)
        carry = ends_h[m:m + 1, :] + ends_p[m:m + 1, :] * carry
    start = jnp.concatenate(starts, axis=0)
    h = jnp.stack([h_loc[j] + p_loc[j] * start for j in range(SEG)], axis=0)
    return h.reshape(PERM, width), carry


def _pipelined_steps(step, proj_a, proj_b):
    s = pl.program_id(0)

    @pl.when(s == 0)
    def _():
        proj_b[...] = jnp.zeros_like(proj_b)

    @pl.when(s % 2 == 0)
    def _():
        step(proj_a, proj_b)

    @pl.when(s % 2 == 1)
    def _():
        step(proj_b, proj_a)


def _starts_sequence(blocks_per_seq):
    s = pl.program_id(0)
    return jnp.logical_or(s == 0, (s + blocks_per_seq - 1) % blocks_per_seq == 0)


def _lru_kernel(x_ref, perm_ref, permt_ref, w_ref, cw_ref, cb_ref, wai_ref, ba_ref, bi_ref, lam_ref, o_ref,
                proj_a, proj_b, halo_ref, h_ref, *, rows, blocks_per_seq):
    @pl.when(_starts_sequence(blocks_per_seq))
    def _():
        halo_ref[...] = jnp.zeros_like(halo_ref)
        h_ref[...] = jnp.zeros_like(h_ref)

    _pipelined_steps(functools.partial(_lru_step, x_ref, perm_ref, permt_ref, w_ref, cw_ref, cb_ref, wai_ref, ba_ref,
                                       bi_ref, lam_ref, o_ref, halo_ref, h_ref, rows), proj_a, proj_b)


def _lru_step(x_ref, perm_ref, permt_ref, w_ref, cw_ref, cb_ref, wai_ref, ba_ref, bi_ref, lam_ref, o_ref, halo_ref,
              h_ref, rows, produce_ref, proj):
    nblk = rows // PERM
    xp = _permute_rows(perm_ref[...], x_ref[...].astype(BF16))
    piece = 2 * D_LRU // LRU_BLOCKS
    cw = cw_ref[...]
    cb = cb_ref[...]
    prev = halo_ref[...]
    xcs = []
    for b in range(nblk):
        xc_b, prev = _perm_conv(proj[b * PERM:(b + 1) * PERM, :D_LRU], prev, cw, cb)
        xcs.append(xc_b)
    halo_ref[...] = prev
    xc = jnp.concatenate(xcs, axis=0)
    xcb = xc.astype(BF16)
    neg_c_sp = (-LRU_C * LOG2E) * _softplus(-lam_ref[...])

    ys = []
    for n in range(LRU_BLOCKS):
        lanes = slice(n * LRU_BLOCK, (n + 1) * LRU_BLOCK)
        ai = jnp.dot(xcb[:, lanes], wai_ref[n], preferred_element_type=F32)
        produce_ref[:, n * piece:(n + 1) * piece] = jnp.dot(xp, w_ref[:, n * piece:(n + 1) * piece],
                                                            preferred_element_type=F32)
        r = _sigmoid(ai[:, :LRU_BLOCK] + ba_ref[:, lanes])
        i = _sigmoid(ai[:, LRU_BLOCK:] + bi_ref[:, lanes])
        a = jnp.exp2(neg_c_sp[:, lanes] * r)
        v = 1.0 - a * a
        u = jnp.where(v > 0.0, v * lax.rsqrt(v), 0.0) * (i * xc[:, lanes])
        carry = h_ref[:, lanes]
        hs = []
        for b in range(nblk):
            h_b, carry = _perm_scan(a[b * PERM:(b + 1) * PERM, :], u[b * PERM:(b + 1) * PERM, :], carry)
            hs.append(h_b)
        h_ref[:, lanes] = carry
        gate = proj[:, D_LRU + n * LRU_BLOCK:D_LRU + (n + 1) * LRU_BLOCK]
        ys.append((_gelu_tanh(gate) * jnp.concatenate(hs, axis=0)).astype(BF16))
    o_ref[...] = _permute_rows(permt_ref[...], jnp.concatenate(ys, axis=1))


def _lru_branch(x, layer, w_in, cw, cb, wai, ba, bi, lam):
    b, s, d = x.shape
    rows = min(LRU_TC, s)
    total = b * s // rows
    perms = _perm_matrices()
    params = (cw, cb, wai, ba, bi, lam)
    args = (x.reshape(total, rows, d),) + perms + (w_in,) + params
    y = pl.pallas_call(
        functools.partial(_lru_kernel, rows=rows, blocks_per_seq=s // rows),
        grid=(total + 1,),
        in_specs=[pl.BlockSpec((None, rows, d), lambda i: (jnp.minimum(i, total - 1), 0, 0))]
        + [_resident(a.shape) for a in perms] + [_layer(w_in, layer, (d, 2 * D_LRU))]
        + [_layer(a, layer) for a in params],
        out_specs=pl.BlockSpec((None, rows, D_LRU), lambda i: (jnp.maximum(i - 1, 0), 0, 0)),
        out_shape=jax.ShapeDtypeStruct((total, rows, D_LRU), BF16),
        scratch_shapes=[pltpu.VMEM((rows, 2 * D_LRU), F32), pltpu.VMEM((rows, 2 * D_LRU), F32),
                        pltpu.VMEM((HALO, D_LRU), F32), pltpu.VMEM((1, D_LRU), F32)],
        compiler_params=pltpu.CompilerParams(dimension_semantics=("arbitrary",), vmem_limit_bytes=VMEM_LIMIT),
        name="lru_branch",
    )(*args)
    return y.reshape(b, s, D_LRU)


def _head_expansion_matrix():
    e = np.zeros((2 * LANES, D_SSD), np.float32)
    for h in range(SSD_HEADS):
        for t in range(2):
            e[t * LANES + h, h * SSD_HEAD_DIM:(h + 1) * SSD_HEAD_DIM] = 1.0
    return jnp.asarray(e, BF16)


def _expand_heads(w, ehead):
    return jnp.dot(_split_bf16(w, 2), ehead, preferred_element_type=F32)


def _ssd_kernel(x_ref, perm_ref, permt_ref, w_ref, cw_ref, cb_ref, dtb_ref, alog_ref, dskip_ref, nw_ref, ehead_ref,
                o_ref, proj_a, proj_b, halo_ref, state_ref, *, rows, blocks_per_seq):
    @pl.when(_starts_sequence(blocks_per_seq))
    def _():
        halo_ref[...] = jnp.zeros_like(halo_ref)
        state_ref[...] = jnp.zeros_like(state_ref)

    _pipelined_steps(functools.partial(_ssd_step, x_ref, perm_ref, permt_ref, w_ref, cw_ref, cb_ref, dtb_ref, alog_ref,
                                       dskip_ref, nw_ref, ehead_ref, o_ref, halo_ref, state_ref, rows),
                     proj_a, proj_b)


def _ssd_step(x_ref, perm_ref, permt_ref, w_ref, cw_ref, cb_ref, dtb_ref, alog_ref, dskip_ref, nw_ref, ehead_ref, o_ref,
              halo_ref, state_ref, rows, produce_ref, proj):
    length = PERM
    xp = _permute_rows(perm_ref[...], x_ref[...].astype(BF16))
    n_pieces = (rows // length) * SSD_GROUPS
    mxu_cols = 2 * LANES
    n_tiles = pl.cdiv(w_ref.shape[1], mxu_cols)
    bounds = [min((i * n_tiles // n_pieces) * mxu_cols, w_ref.shape[1]) for i in range(n_pieces)] + [w_ref.shape[1]]
    dt = _softplus(proj[:, D_SSD + D_XBC:] + dtb_ref[...])
    da2 = dt * (-LOG2E * jnp.exp(alog_ref[...]))
    r_i = lax.broadcasted_iota(jnp.int32, (length, length), 0)
    c_i = lax.broadcasted_iota(jnp.int32, (length, length), 1)
    tok = lambda p: SEG * (p & (SUBLANES - 1)) + (p >> 3)
    causal = tok(r_i) >= tok(c_i)
    neg_mask = jnp.where(causal, 0.0, NON_CAUSAL_LOG2)
    lane = lax.broadcasted_iota(jnp.int32, (length, LANES), 1)
    first_head = lane < SSD_HEAD_DIM
    ehead = ehead_ref[...]
    cw = cw_ref[...]
    cb_conv = cb_ref[...]
    prev = halo_ref[...]

    for c in range(rows // length):
        rs = slice(c * length, (c + 1) * length)
        conv, prev = _perm_conv(proj[rs, D_SSD:D_SSD + D_XBC], prev, cw, cb_conv)
        xbc = _silu(conv)
        xs = xbc[:, :D_SSD]
        dt_c = dt[rs, :]
        b_t = [xbc[:, D_SSD + g * SSD_STATE:D_SSD + (g + 1) * SSD_STATE].T.astype(BF16) for g in range(SSD_GROUPS)]
        c_all = [xbc[:, D_SSD + D_BC + g * SSD_STATE:D_SSD + D_BC + (g + 1) * SSD_STATE].astype(BF16)
                 for g in range(SSD_GROUPS)]
        cb_all = [jnp.dot(c_all[g], b_t[g], preferred_element_type=F32) for g in range(SSD_GROUPS)]
        csl = _perm_cumsum(da2[rs, :])
        w_state = dt_c * jnp.exp2(csl[length - 1:length, :] - csl)
        w_out = jnp.exp2(csl)
        src_t = (csl - jnp.log2(dt_c)).T
        w_state_e = _expand_heads(w_state, ehead)
        w_out_e = _expand_heads(w_out, ehead)

        y_cols = []
        for g in range(SSD_GROUPS):
            lo, hi = bounds[c * SSD_GROUPS + g], bounds[c * SSD_GROUPS + g + 1]
            produce_ref[:, lo:hi] = jnp.dot(xp, w_ref[:, lo:hi], preferred_element_type=F32)
            tiles = []
            c_g = c_all[g]
            cb = cb_all[g]
            gl = slice(g * D_GROUP, (g + 1) * D_GROUP)
            heads_per_group = SSD_HEADS // SSD_GROUPS
            for p in range(heads_per_group // 2):
                h0 = g * heads_per_group + 2 * p
                ms = []
                for h in (h0, h0 + 1):
                    seg = jnp.broadcast_to(csl[:, h:h + 1], (length, length)) - src_t[h:h + 1, :] + neg_mask
                    ms.append((cb * jnp.exp2(seg)).astype(BF16))
                x_t = xs[:, h0 * SSD_HEAD_DIM:(h0 + 2) * SSD_HEAD_DIM]
                rhs = jnp.concatenate([jnp.where(first_head, x_t, 0.0), jnp.where(first_head, 0.0, x_t)],
                                      axis=0).astype(BF16)
                tiles.append(jnp.dot(jnp.concatenate(ms, axis=1), rhs, preferred_element_type=F32))
            st = state_ref[g]
            y_off = jnp.dot(c_g, st.astype(BF16), preferred_element_type=F32) * w_out_e[:, gl]
            xd = (xs[:, gl] * w_state_e[:, gl]).astype(BF16)
            s_new = jnp.dot(b_t[g], xd, preferred_element_type=F32)
            state_ref[g] = st * w_out_e[length - 1:length, gl] + s_new
            yg = jnp.concatenate(tiles, axis=1) + y_off + dskip_ref[:, gl] * xs[:, gl]
            yg = yg * _silu(proj[rs, g * D_GROUP:(g + 1) * D_GROUP])
            yg = yg * lax.rsqrt(jnp.mean(yg * yg, axis=-1, keepdims=True) + EPS)
            y_cols.append((yg * nw_ref[:, gl]).astype(BF16))
        o_ref[rs, :] = _permute_rows(permt_ref[...], jnp.concatenate(y_cols, axis=1))
    halo_ref[...] = prev


def _ssd_branch(x, layer, w, cw, cb, dtb, alog, dskip, nw):
    b, s, d = x.shape
    rows = min(SSD_TC, s)
    total = b * s // rows
    width = w.shape[2]
    perms = _perm_matrices()
    params = (w, cw, cb, dtb, alog, dskip, nw)
    ehead = _head_expansion_matrix()
    args = (x.reshape(total, rows, d),) + perms + params + (ehead,)
    y = pl.pallas_call(
        functools.partial(_ssd_kernel, rows=rows, blocks_per_seq=s // rows),
        grid=(total + 1,),
        in_specs=[pl.BlockSpec((None, rows, d), lambda i: (jnp.minimum(i, total - 1), 0, 0))]
        + [_resident(a.shape) for a in perms] + [_layer(a, layer) for a in params] + [_resident(ehead.shape)],
        out_specs=pl.BlockSpec((None, rows, D_SSD), lambda i: (jnp.maximum(i - 1, 0), 0, 0)),
        out_shape=jax.ShapeDtypeStruct((total, rows, D_SSD), BF16),
        scratch_shapes=[pltpu.VMEM((rows, width), F32), pltpu.VMEM((rows, width), F32),
                        pltpu.VMEM((HALO, D_XBC), F32), pltpu.VMEM((SSD_GROUPS, SSD_STATE, D_GROUP), F32)],
        compiler_params=pltpu.CompilerParams(dimension_semantics=("arbitrary",), vmem_limit_bytes=VMEM_LIMIT),
        name="ssd_branch",
    )(*args)
    return y.reshape(b, s, D_SSD)


def _mix_kernel(x_ref, wqg_ref, kv_ref, ylru_ref, yssd_ref, bg_ref, wl_ref, ws_ref, wx_ref, wo_ref, g_ref, b_ref,
                o_ref, *, alpha, sub):
    dot = functools.partial(jnp.dot, preferred_element_type=F32)
    blocks = [slice(i * sub, (i + 1) * sub) for i in range(x_ref.shape[0] // sub)]
    xf = [x_ref[r, :] for r in blocks]
    pq = [dot(x.astype(BF16), wqg_ref[...]) for x in xf]
    scores = []
    for p in pq:
        for h in range(XA_HEADS):
            q_h = p[:, h * XA_HEAD_DIM:(h + 1) * XA_HEAD_DIM].astype(BF16)
            k_h = kv_ref[:, h * XA_HEAD_DIM:(h + 1) * XA_HEAD_DIM]
            scores.append(lax.dot_general(q_h, k_h, (((1,), (1,)), ((), ())), preferred_element_type=F32))
    br_lru = [dot(ylru_ref[r, :], wl_ref[...]) for r in blocks]
    br_ssd = [dot(yssd_ref[r, :], ws_ref[...]) for r in blocks]
    outs = []
    for sc in scores:
        e = jnp.exp2((sc - jnp.max(sc, axis=-1, keepdims=True)) * (LOG2E * XA_HEAD_DIM ** -0.5))
        probs = (e / jnp.sum(e, axis=-1, keepdims=True)).astype(BF16)
        h = len(outs) % XA_HEADS
        outs.append(dot(probs, kv_ref[:, D_XA + h * XA_HEAD_DIM:D_XA + (h + 1) * XA_HEAD_DIM]))
    br_xa = [dot(jnp.concatenate(outs[i * XA_HEADS:(i + 1) * XA_HEADS], axis=1).astype(BF16), wx_ref[...])
             for i in range(len(blocks))]
    merged = []
    for i, p in enumerate(pq):
        gates = _sigmoid(p[:, D_XA:] + bg_ref[...])
        merged.append((gates[:, :D_MODEL] * br_lru[i] + gates[:, D_MODEL:2 * D_MODEL] * br_ssd[i]
                       + gates[:, 2 * D_MODEL:] * br_xa[i]).astype(BF16))
    mix = [dot(m, wo_ref[...]) for m in merged]
    for i, r in enumerate(blocks):
        o_ref[r, :] = _layer_norm(alpha * xf[i] + mix[i], g_ref[...], b_ref[...])


def _mix_layer(x, layer, wqg, kv, ylru, yssd, bg, wl, ws, wx, wo, g, bb, alpha):
    b, s, d = x.shape
    rows = min(MIX_TM, s)
    m = kv.shape[1]
    tok = lambda width: pl.BlockSpec((None, rows, width), lambda i, j: (i, j, 0))
    return pl.pallas_call(
        functools.partial(_mix_kernel, alpha=alpha, sub=min(MIX_SUB, rows)),
        grid=(b, s // rows),
        in_specs=[tok(d), _layer(wqg, layer), pl.BlockSpec((None, m, kv.shape[2]), lambda i, j: (i, 0, 0)),
                  tok(D_LRU), tok(D_SSD)] + [_layer(a, layer) for a in (bg, wl, ws, wx, wo, g, bb)],
        out_specs=tok(d),
        out_shape=jax.ShapeDtypeStruct((b, s, d), F32),
        compiler_params=pltpu.CompilerParams(dimension_semantics=("arbitrary", "arbitrary"),
                                             vmem_limit_bytes=VMEM_LIMIT),
        name="mix_layer",
    )(x, wqg, kv, ylru, yssd, bg, wl, ws, wx, wo, g, bb)


def _ffn_kernel(x_ref, wi_ref, wd_ref, g_ref, b_ref, o_ref, *, alpha, sub):
    dot = functools.partial(jnp.dot, preferred_element_type=F32)
    blocks = [slice(i * sub, (i + 1) * sub) for i in range(x_ref.shape[0] // sub)]
    xf = [x_ref[r, :] for r in blocks]
    hid = [dot(x.astype(BF16), wi_ref[...]) for x in xf]
    down = [dot((_silu(h[:, :D_FF]) * h[:, D_FF:]).astype(BF16), wd_ref[...]) for h in hid]
    for i, r in enumerate(blocks):
        o_ref[r, :] = _layer_norm(alpha * xf[i] + down[i], g_ref[...], b_ref[...])


def _ffn_layer(x, layer, wi, wd, g, bb, alpha):
    b, s, d = x.shape
    rows = min(FFN_TM, s)
    tok = pl.BlockSpec((None, rows, d), lambda i, j: (i, j, 0))
    return pl.pallas_call(
        functools.partial(_ffn_kernel, alpha=alpha, sub=min(FFN_SUB, rows)),
        grid=(b, s // rows),
        in_specs=[tok] + [_layer(a, layer) for a in (wi, wd, g, bb)],
        out_specs=tok,
        out_shape=jax.ShapeDtypeStruct((b, s, d), F32),
        compiler_params=pltpu.CompilerParams(dimension_semantics=("arbitrary", "arbitrary"),
                                             vmem_limit_bytes=VMEM_LIMIT),
        name="ffn_layer",
    )(x, wi, wd, g, bb)


def kernel(x, mem, w_in, b_gate, lru_conv_w, lru_conv_b, lru_w_a, lru_b_a, lru_w_i, lru_b_i, lru_lambda, ssd_conv_w,
           ssd_conv_b, ssd_dt_bias, ssd_a_log, ssd_d, ssd_norm_w, mem_w_kv, w_br_lru, w_br_ssd, w_br_xa, w_out, ln1_g,
           ln1_b, ffn_w_in, ffn_w_down, ln2_g, ln2_b):
    depth = w_in.shape[0]
    alpha = (2 * depth) ** 0.25
    o_z = 2 * D_LRU
    o_xbc = o_z + D_SSD
    o_dt = o_xbc + D_XBC
    o_q = o_dt + SSD_HEADS

    w_in_bf = w_in.astype(BF16)
    w_ssd = w_in_bf[:, :, o_z:o_dt + DT_PAD]
    w_qg = w_in_bf[:, :, o_q:]
    w_ai = jnp.concatenate([lru_w_a, lru_w_i], axis=-1).astype(BF16)
    pad_heads = lambda a: jnp.pad(a, ((0, 0), (0, DT_PAD - SSD_HEADS)))[:, None, :]
    dtb = pad_heads(ssd_dt_bias)
    alog = pad_heads(ssd_a_log)
    dskip = jnp.repeat(ssd_d, SSD_HEAD_DIM, axis=-1)[:, None, :]
    row = lambda a: a[:, None, :]
    w_kv = mem_w_kv.astype(BF16)
    wl, ws, wx, wo = (w.astype(BF16) for w in (w_br_lru, w_br_ssd, w_br_xa, w_out))
    wi, wd = ffn_w_in.astype(BF16), ffn_w_down.astype(BF16)
    bg = b_gate.reshape(depth, 1, N_BRANCH * D_MODEL)
    lru_params = (lru_conv_w, row(lru_conv_b), w_ai, row(lru_b_a), row(lru_b_i), row(lru_lambda))
    ssd_params = (w_ssd, ssd_conv_w, row(ssd_conv_b), dtb, alog, dskip, row(ssd_norm_w))
    ln1 = (row(ln1_g), row(ln1_b))
    ln2 = (row(ln2_g), row(ln2_b))

    for l in range(depth):
        kv = _kv_proj(mem, w_kv, l)
        y_lru = _lru_branch(x, l, w_in_bf, *lru_params)
        y_ssd = _ssd_branch(x, l, *ssd_params)
        x = _mix_layer(x, l, w_qg, kv, y_lru, y_ssd, bg, wl, ws, wx, wo, *ln1, alpha)
        x = _ffn_layer(x, l, wi, wd, *ln2, alpha)
    return x
```
